```python
import math
import jax, jax.numpy as jnp
from jax import lax
import numpy as np

D_MODEL = 2048
BATCH = 16
SEQ = 2048
DEPTH = 2

EXPAND = 2
D_INNER = EXPAND * D_MODEL
N_MIXERS = 2
N_A = (DEPTH + 1) // 2
N_B = DEPTH // 2
N_HEADS = 32
HEAD_DIM = D_INNER // N_HEADS
KV_LORA = 256
IDX_HEADS = 16
IDX_DIM = 64
TOP_K_MAX = 256
Q_BLOCK = 128
A_SIZES = (D_INNER, KV_LORA, IDX_HEADS * IDX_DIM, IDX_DIM, IDX_HEADS, D_INNER)
A_IN = D_INNER + KV_LORA + IDX_HEADS * IDX_DIM + IDX_DIM + IDX_HEADS + D_INNER
A_SPLITS = (D_INNER,
            D_INNER + KV_LORA,
            D_INNER + KV_LORA + IDX_HEADS * IDX_DIM,
            D_INNER + KV_LORA + IDX_HEADS * IDX_DIM + IDX_DIM,
            D_INNER + KV_LORA + IDX_HEADS * IDX_DIM + IDX_DIM + IDX_HEADS)
REL_BUCKETS = 32
REL_MAX_DIST = 128
POOL_WINDOWS = (2, 4, 8, 16)
N_POOL_GROUPS = 4
POOL_GROUP = D_INNER // N_POOL_GROUPS
EPS = 1e-6

kernel_name = "hybrid_dsa_pool_interleaved"


def rmsnorm(x, g):
    xf = x.astype(jnp.float32)
    y = xf * lax.rsqrt(jnp.mean(xf * xf, axis=-1, keepdims=True) + EPS)
    return (y * g.astype(jnp.float32)).astype(x.dtype)


def t5_bucket(dist):
    max_exact = REL_BUCKETS // 2
    d = jnp.maximum(dist, 0)
    df = jnp.maximum(d, 1).astype(jnp.float32)
    large = max_exact + (jnp.log(df / max_exact) / math.log(REL_MAX_DIST / max_exact)
                         * (REL_BUCKETS - max_exact)).astype(jnp.int32)
    large = jnp.minimum(large, REL_BUCKETS - 1)
    return jnp.where(d < max_exact, d, large)


def dsa_mixer(h, w_in, kv_norm, kidx_norm, w_uk, w_uv, w_out, rel_bias):
    B, L, _ = h.shape
    proj = h @ w_in
    q, c_kv, q_idx, k_idx, w_idx, z = jnp.split(proj, list(A_SPLITS), axis=-1)
    q = q.reshape(B, L, N_HEADS, HEAD_DIM)
    c_kv = rmsnorm(c_kv, kv_norm)
    q_idx = q_idx.reshape(B, L, IDX_HEADS, IDX_DIM)
    k_idx = rmsnorm(k_idx, kidx_norm).astype(jnp.float32)
    w_idx = w_idx * (IDX_HEADS ** -0.5)
    top_k = min(TOP_K_MAX, L // 4)
    n_blk = L // Q_BLOCK
    key_pos = jnp.arange(L)

    def to_blocks(a):
        return a.reshape(B, n_blk, Q_BLOCK, *a.shape[2:]).swapaxes(0, 1)

    def block_fn(args):
        blk, qb, qib, wb = args
        t_pos = blk * Q_BLOCK + jnp.arange(Q_BLOCK)
        s = jnp.einsum("bthd,bsd->bths", qib.astype(jnp.float32), k_idx) * (IDX_DIM ** -0.5)
        score = jnp.einsum("bth,bths->bts", wb.astype(jnp.float32), jax.nn.relu(s))
        causal = key_pos[None, :] <= t_pos[:, None]
        score = jnp.where(causal[None], score, -jnp.inf)
        _, idx = lax.top_k(score, top_k)
        valid = idx <= t_pos[None, :, None]
        c_sel = jax.vmap(lambda c, i: c[i])(c_kv, idx)
        q_lat = jnp.einsum("bthd,chd->bthc", qb, w_uk)
        logits = jnp.einsum("bthc,btkc->bthk", q_lat, c_sel).astype(jnp.float32) * (HEAD_DIM ** -0.5)
        bias = rel_bias[t5_bucket(t_pos[None, :, None] - idx)]
        logits = logits + jnp.swapaxes(bias, 2, 3).astype(jnp.float32)
        logits = jnp.where(valid[:, :, None, :], logits, -jnp.inf)
        p = jax.nn.softmax(logits, axis=-1).astype(c_sel.dtype)
        o_lat = jnp.einsum("bthk,btkc->bthc", p, c_sel)
        return jnp.einsum("bthc,chd->bthd", o_lat, w_uv)

    out = lax.map(block_fn, (jnp.arange(n_blk), to_blocks(q), to_blocks(q_idx), to_blocks(w_idx)))
    out = out.swapaxes(0, 1).reshape(B, L, D_INNER)
    y = out * jax.nn.silu(z)
    return y @ w_out


def pool_mixer(h, w_in, w_grp, b_grp, scale, w_out):
    B, L, _ = h.shape
    u, z = jnp.split(h @ w_in, 2, axis=-1)
    ug = u.reshape(B, L, N_POOL_GROUPS, POOL_GROUP).astype(jnp.float32)
    cs = jnp.concatenate([jnp.zeros_like(ug[:, :1]), jnp.cumsum(ug, axis=1)], axis=1)
    pos = jnp.arange(L)
    win = jnp.array(POOL_WINDOWS, dtype=jnp.int32)
    start = jnp.maximum(pos[:, None] - win[None, :] + 1, 0)
    cnt = (pos[:, None] - start + 1).astype(jnp.float32)
    lo = cs[:, start, jnp.arange(N_POOL_GROUPS)]
    pooled = (cs[:, 1:] - lo) / cnt[None, :, :, None] - ug
    mixed = jnp.einsum("blgp,gpq->blgq", pooled, w_grp.astype(jnp.float32)) + b_grp.astype(jnp.float32)
    mixed = mixed.reshape(B, L, D_INNER) * scale.astype(jnp.float32)
    y = mixed.astype(h.dtype) * jax.nn.silu(z)
    return y @ w_out


def setup_inputs(seed: int = 0) -> dict:
    key = jax.random.key(seed)
    ks = jax.random.split(key, 20)
    f32 = jnp.float32
    nrm = lambda k, shp, s: jax.random.normal(k, shp, f32) * s
    return {
        "x": nrm(ks[0], (BATCH, SEQ, D_MODEL), 1.0),
        "norm_a": 1.0 + nrm(ks[1], (N_A, D_MODEL), 0.05),
        "w_in_a": nrm(ks[2], (N_A, D_MODEL, A_IN), D_MODEL ** -0.5),
        "kv_norm_a": 1.0 + nrm(ks[3], (N_A, KV_LORA), 0.05),
        "kidx_norm_a": 1.0 + nrm(ks[4], (N_A, IDX_DIM), 0.05),
        "w_uk_a": nrm(ks[5], (N_A, KV_LORA, N_HEADS, HEAD_DIM), KV_LORA ** -0.5),
        "w_uv_a": nrm(ks[6], (N_A, KV_LORA, N_HEADS, HEAD_DIM), KV_LORA ** -0.5),
        "w_out_a": nrm(ks[7], (N_A, D_INNER, D_MODEL), D_INNER ** -0.5),
        "norm_b": 1.0 + nrm(ks[8], (N_B, D_MODEL), 0.05),
        "w_in_b": nrm(ks[9], (N_B, D_MODEL, 2 * D_INNER), D_MODEL ** -0.5),
        "w_grp_b": nrm(ks[10], (N_B, N_POOL_GROUPS, POOL_GROUP, POOL_GROUP), POOL_GROUP ** -0.5),
        "b_grp_b": nrm(ks[11], (N_B, N_POOL_GROUPS, POOL_GROUP), 0.02),
        "scale_b": 1.0 + nrm(ks[12], (N_B, D_INNER), 0.1),
        "w_out_b": nrm(ks[13], (N_B, D_INNER, D_MODEL), D_INNER ** -0.5),
        "rel_bias": nrm(ks[14], (REL_BUCKETS, N_HEADS), 0.5),
        "final_norm": 1.0 + nrm(ks[15], (D_MODEL,), 0.05),
    }


def reference(x, norm_a, w_in_a, kv_norm_a, kidx_norm_a, w_uk_a, w_uv_a, w_out_a,
              norm_b, w_in_b, w_grp_b, b_grp_b, scale_b, w_out_b, rel_bias, final_norm):
    h = x
    for i in range(DEPTH):
        j = i // N_MIXERS
        if i % N_MIXERS == 0:
            h = h + dsa_mixer(rmsnorm(h, norm_a[j]), w_in_a[j], kv_norm_a[j], kidx_norm_a[j],
                              w_uk_a[j], w_uv_a[j], w_out_a[j], rel_bias)
        else:
            h = h + pool_mixer(rmsnorm(h, norm_b[j]), w_in_b[j], w_grp_b[j], b_grp_b[j],
                               scale_b[j], w_out_b[j])
    return rmsnorm(h, final_norm)
```

```python
import functools
import math

import numpy as np
import jax
import jax.numpy as jnp
from jax import lax
from jax.experimental import pallas as pl
from jax.experimental.pallas import tpu as pltpu

F32 = jnp.float32
BF16 = jnp.bfloat16

EPS = 1e-6
Q_BLOCK = 128
TOP_K_MAX = 256
REL_BUCKETS = 32
REL_MAX_DIST = 128
POOL_WINDOWS = (2, 4, 8, 16)
POOL_HALO = 16
NEG = -1e30
INT_MIN = -2 ** 31
NEAR_SUB = 3
VMEM_LIMIT = 56 * 1024 * 1024


def _cparams(n_grid):
    return pltpu.CompilerParams(dimension_semantics=("arbitrary",) * n_grid,
                                vmem_limit_bytes=VMEM_LIMIT)


def _silu(z):
    return z * (1.0 / (1.0 + jnp.exp(-z)))


def _norm_matmul_kernel(x_ref, g_ref, w_ref, o_ref, xn_ref):
    @pl.when(pl.program_id(1) == 0)
    def _():
        x = x_ref[...]
        ms = jnp.mean(x * x, axis=-1, keepdims=True)
        xn_ref[...] = (x * lax.rsqrt(ms + EPS) * g_ref[...]).astype(BF16)

    o_ref[...] = jnp.dot(xn_ref[...], w_ref[...],
                         preferred_element_type=F32).astype(o_ref.dtype)


def _norm_matmul(x, g, w, out_dtype, tm, tn, name):
    m, k = x.shape
    n = w.shape[1]
    tn = min(tn, n)
    return pl.pallas_call(
        _norm_matmul_kernel,
        grid=(m // tm, n // tn),
        in_specs=[pl.BlockSpec((tm, k), lambda i, j: (i, 0)),
                  pl.BlockSpec((1, k), lambda i, j: (0, 0)),
                  pl.BlockSpec((k, tn), lambda i, j: (0, j))],
        out_specs=pl.BlockSpec((tm, tn), lambda i, j: (i, j)),
        out_shape=jax.ShapeDtypeStruct((m, n), out_dtype),
        scratch_shapes=[pltpu.VMEM((tm, k), BF16)],
        compiler_params=_cparams(2),
        name=name,
    )(x, g.reshape(1, k), w)


def _matmul_res_kernel(y_ref, w_ref, r_ref, o_ref):
    o_ref[...] = jnp.dot(y_ref[...], w_ref[...], preferred_element_type=F32) + r_ref[...]


def _matmul_res(y, w, r, tm, tn, name):
    m, k = y.shape
    n = w.shape[1]
    tn = min(tn, n)
    return pl.pallas_call(
        _matmul_res_kernel,
        grid=(n // tn, m // tm),
        in_specs=[pl.BlockSpec((tm, k), lambda j, i: (i, 0)),
                  pl.BlockSpec((k, tn), lambda j, i: (0, j)),
                  pl.BlockSpec((tm, tn), lambda j, i: (i, j))],
        out_specs=pl.BlockSpec((tm, tn), lambda j, i: (i, j)),
        out_shape=jax.ShapeDtypeStruct((m, n), F32),
        compiler_params=_cparams(2),
        name=name,
    )(y, w, r)


def _rmsnorm_kernel(x_ref, g_ref, o_ref):
    x = x_ref[...]
    ms = jnp.mean(x * x, axis=-1, keepdims=True)
    o_ref[...] = x * lax.rsqrt(ms + EPS) * g_ref[...]


def _rmsnorm(x, g, tm, name):
    m, k = x.shape
    return pl.pallas_call(
        _rmsnorm_kernel,
        grid=(m // tm,),
        in_specs=[pl.BlockSpec((tm, k), lambda i: (i, 0)),
                  pl.BlockSpec((1, k), lambda i: (0, 0))],
        out_specs=pl.BlockSpec((tm, k), lambda i: (i, 0)),
        out_shape=jax.ShapeDtypeStruct((m, k), F32),
        compiler_params=_cparams(1),
        name=name,
    )(x, g.reshape(1, k))


def _latent_kernel(sm_ref, kvn_ref, kin_ref, cn_ref, kn_ref, *, kv_lora, idx_dim):
    x = sm_ref[...]
    c = x[:, :kv_lora]
    k = x[:, kv_lora:kv_lora + idx_dim]
    cn = c * lax.rsqrt(jnp.mean(c * c, axis=-1, keepdims=True) + EPS) * kvn_ref[...]
    kn = k * lax.rsqrt(jnp.mean(k * k, axis=-1, keepdims=True) + EPS) * kin_ref[...]
    cn_ref[...] = cn.astype(BF16)
    kn_ref[...] = kn.astype(BF16)


def _latent_prep(small, kv_norm, kidx_norm, tm):
    m, ws = small.shape
    kv_lora = kv_norm.shape[0]
    idx_dim = kidx_norm.shape[0]
    return pl.pallas_call(
        functools.partial(_latent_kernel, kv_lora=kv_lora, idx_dim=idx_dim),
        grid=(m // tm,),
        in_specs=[pl.BlockSpec((tm, ws), lambda i: (i, 0)),
                  pl.BlockSpec((1, kv_lora), lambda i: (0, 0)),
                  pl.BlockSpec((1, idx_dim), lambda i: (0, 0))],
        out_specs=[pl.BlockSpec((tm, kv_lora), lambda i: (i, 0)),
                   pl.BlockSpec((tm, idx_dim), lambda i: (i, 0))],
        out_shape=[jax.ShapeDtypeStruct((m, kv_lora), BF16),
                   jax.ShapeDtypeStruct((m, idx_dim), BF16)],
        compiler_params=_cparams(1),
        name="latent_prep",
    )(small, kv_norm.reshape(1, kv_lora), kidx_norm.reshape(1, idx_dim))


def _indexer_kernel(qi_ref, sm_ref, kn_ref, mask_ref, *, top_k, n_heads, idx_dim, w_off):
    i = pl.program_id(1)
    seq = kn_ref.shape[1]
    n_sub = seq // Q_BLOCK
    q = qi_ref[0]
    w = sm_ref[0][:, w_off:w_off + n_heads] * (n_heads ** -0.5 * idx_dim ** -0.5)
    kn = kn_ref[0]

    score = jnp.zeros((Q_BLOCK, seq), F32)
    for h in range(n_heads):
        qh = q[:, h * idx_dim:(h + 1) * idx_dim].astype(BF16)
        s = lax.dot_general(qh, kn, (((1,), (1,)), ((), ())), preferred_element_type=F32)
        score = score + w[:, h:h + 1] * jnp.maximum(s, 0.0)

    bits = pltpu.bitcast(score + 0.0, jnp.int32)
    key = jnp.where(bits >= 0, bits, bits ^ jnp.int32(0x7FFFFFFF))
    t_pos = i * Q_BLOCK + lax.broadcasted_iota(jnp.int32, (Q_BLOCK, seq), 0)
    s_pos = lax.broadcasted_iota(jnp.int32, (Q_BLOCK, seq), 1)
    causal = s_pos <= t_pos
    key = jnp.where(causal, key, jnp.int32(INT_MIN))

    def count_ge(thr):
        return jnp.sum(jnp.where(key >= thr, 1.0, 0.0), axis=-1, keepdims=True)

    kf = float(top_k)
    thr = jnp.where(count_ge(jnp.zeros((Q_BLOCK, 1), jnp.int32)) >= kf,
                    jnp.int32(0), jnp.int32(INT_MIN))

    def bit_step(b, thr):
        cand = thr + jnp.left_shift(jnp.int32(1), 30 - b)
        return jnp.where(count_ge(cand) >= kf, cand, thr)

    thr = lax.fori_loop(0, 31, bit_step, thr)

    ge = key >= thr
    sel = jnp.where(causal, jnp.where(ge, 1.0, 0.0), 0.0)
    for j in range(n_sub):
        blk = sel[:, j * Q_BLOCK:(j + 1) * Q_BLOCK]
        mask_ref[0, 0, j] = jnp.where(blk > 0.5, 0.0, NEG).astype(BF16)

    n_ge = jnp.sum(jnp.where(ge, 1.0, 0.0), axis=-1, keepdims=True)
    has_tie = jnp.logical_and(n_ge > kf, thr > jnp.int32(INT_MIN))
    any_tie = jnp.max(jnp.where(has_tie, 1.0, 0.0))

    @pl.when(any_tie > 0.0)
    def _():
        cw = 2 * Q_BLOCK
        gt = key > thr
        eq = key == thr
        need = kf - jnp.sum(jnp.where(gt, 1.0, 0.0), axis=-1, keepdims=True)
        r_id = lax.broadcasted_iota(jnp.int32, (cw, cw), 0)
        c_id = lax.broadcasted_iota(jnp.int32, (cw, cw), 1)
        upper = jnp.where(r_id < c_id, 1.0, 0.0).astype(BF16)
        carry = jnp.zeros((Q_BLOCK, 1), F32)
        for c in range(seq // cw):
            sl = slice(c * cw, (c + 1) * cw)
            eqc = jnp.where(eq[:, sl], 1.0, 0.0)
            rank = carry + jnp.dot(eqc.astype(BF16), upper, preferred_element_type=F32)
            keep = jnp.where(gt[:, sl], 1.0, jnp.where(rank < need, eqc, 0.0))
            keep = jnp.where(has_tie, keep, sel[:, sl])
            carry = carry + jnp.sum(eqc, axis=-1, keepdims=True)
            add = jnp.where(keep > 0.5, 0.0, NEG).astype(BF16)
            mask_ref[0, 0, 2 * c] = add[:, :Q_BLOCK]
            mask_ref[0, 0, 2 * c + 1] = add[:, Q_BLOCK:]


def _indexer(qi, small, kn, top_k, n_heads, idx_dim, w_off):
    b, seq, _ = qi.shape
    n_blk = seq // Q_BLOCK
    return pl.pallas_call(
        functools.partial(_indexer_kernel, top_k=top_k, n_heads=n_heads, idx_dim=idx_dim,
                          w_off=w_off),
        grid=(b, n_blk),
        in_specs=[pl.BlockSpec((1, Q_BLOCK, qi.shape[2]), lambda bb, i: (bb, i, 0)),
                  pl.BlockSpec((1, Q_BLOCK, small.shape[2]), lambda bb, i: (bb, i, 0)),
                  pl.BlockSpec((1, seq, idx_dim), lambda bb, i: (bb, 0, 0))],
        out_specs=pl.BlockSpec((1, 1, n_blk, Q_BLOCK, Q_BLOCK), lambda bb, i: (bb, i, 0, 0, 0)),
        out_shape=jax.ShapeDtypeStruct((b, n_blk, n_blk, Q_BLOCK, Q_BLOCK), BF16),
        compiler_params=_cparams(2),
        name="indexer_mask",
    )(qi, small, kn)


def _attn_kernel(q_ref, z_ref, cn_ref, mask_ref, wuk_ref, wuv_ref, bias_ref, y_ref,
                 qlat_ref, acc_ref, m_ref, l_ref, *, n_grp, head_dim, scale):
    i = pl.program_id(2)
    rows = n_grp * Q_BLOCK
    kv_lora = cn_ref.shape[2]

    for h in range(n_grp):
        qh = q_ref[0, :, h * head_dim:(h + 1) * head_dim]
        ql = jnp.dot(qh, wuk_ref[h], preferred_element_type=F32) * scale
        qlat_ref[h * Q_BLOCK:(h + 1) * Q_BLOCK, :] = ql.astype(BF16)
    m_ref[...] = jnp.full(m_ref.shape, NEG, F32)
    l_ref[...] = jnp.zeros(l_ref.shape, F32)
    acc_ref[...] = jnp.zeros(acc_ref.shape, F32)

    def chunk(sub0, n_sb, bias_lo):
        width = n_sb * Q_BLOCK
        kstart = pl.multiple_of(sub0 * Q_BLOCK, Q_BLOCK)
        kc = cn_ref[0, pl.ds(kstart, width), :]
        s = lax.dot_general(qlat_ref[...], kc, (((1,), (1,)), ((), ())),
                            preferred_element_type=F32)
        mk = jnp.concatenate([mask_ref[0, 0, sub0 + j] for j in range(n_sb)],
                             axis=-1).astype(F32)
        s3 = s.reshape(n_grp, Q_BLOCK, width) + mk[None]
        if bias_lo is not None:
            s3 = s3 + bias_ref[:, :, bias_lo:bias_lo + width]
        s = s3.reshape(rows, width)
        m_prev = m_ref[...]
        m_new = jnp.maximum(m_prev, jnp.max(s, axis=-1, keepdims=True))
        alpha = jnp.exp(m_prev - m_new)
        p = jnp.exp(s - m_new[:, :1])
        l_ref[...] = alpha * l_ref[...] + jnp.sum(p, axis=-1, keepdims=True)
        acc_ref[...] = acc_ref[...] * alpha[:, :1] + jnp.dot(
            p.astype(BF16), kc, preferred_element_type=F32)
        m_ref[...] = m_new

    def far_step(j, carry):
        chunk(2 * j, 2, None)
        return carry

    lax.fori_loop(0, jnp.maximum(i - 1, 0) // 2, far_step, 0)

    @pl.when(i == 0)
    def _():
        chunk(0, 1, 2 * Q_BLOCK)

    @pl.when(i % 2 == 1)
    def _():
        chunk(i - 1, 2, Q_BLOCK)

    @pl.when(jnp.logical_and(i % 2 == 0, i > 0))
    def _():
        chunk(i - 2, 3, 0)

    o_lat = acc_ref[...] / l_ref[...][:, :1]
    for h in range(n_grp):
        oh = jnp.dot(o_lat[h * Q_BLOCK:(h + 1) * Q_BLOCK].astype(BF16), wuv_ref[h],
                     preferred_element_type=F32)
        zz = z_ref[0, :, h * head_dim:(h + 1) * head_dim].astype(F32)
        y_ref[0, :, h * head_dim:(h + 1) * head_dim] = (oh * _silu(zz)).astype(BF16)


def _attention(qz, cn, mask, wuk_t, wuv_h, bias_tile, n_grp):
    b, seq, two_inner = qz.shape
    d_inner = two_inner // 2
    n_heads, head_dim, kv_lora = wuk_t.shape
    n_blk = seq // Q_BLOCK
    n_g = n_heads // n_grp
    gw = n_grp * head_dim
    rows = n_grp * Q_BLOCK
    return pl.pallas_call(
        functools.partial(_attn_kernel, n_grp=n_grp, head_dim=head_dim, scale=head_dim ** -0.5),
        grid=(n_g, b, n_blk),
        in_specs=[pl.BlockSpec((1, Q_BLOCK, gw), lambda g, bb, i: (bb, i, g)),
                  pl.BlockSpec((1, Q_BLOCK, gw), lambda g, bb, i: (bb, i, n_g + g)),
                  pl.BlockSpec((1, seq, kv_lora), lambda g, bb, i: (bb, 0, 0)),
                  pl.BlockSpec((1, 1, n_blk, Q_BLOCK, Q_BLOCK), lambda g, bb, i: (bb, i, 0, 0, 0)),
                  pl.BlockSpec((n_grp, head_dim, kv_lora), lambda g, bb, i: (g, 0, 0)),
                  pl.BlockSpec((n_grp, kv_lora, head_dim), lambda g, bb, i: (g, 0, 0)),
                  pl.BlockSpec((n_grp, Q_BLOCK, NEAR_SUB * Q_BLOCK), lambda g, bb, i: (g, 0, 0))],
        out_specs=pl.BlockSpec((1, Q_BLOCK, gw), lambda g, bb, i: (bb, i, g)),
        out_shape=jax.ShapeDtypeStruct((b, seq, d_inner), BF16),
        scratch_shapes=[pltpu.VMEM((rows, kv_lora), BF16),
                        pltpu.VMEM((rows, kv_lora), F32),
                        pltpu.VMEM((rows, Q_BLOCK), F32),
                        pltpu.VMEM((rows, Q_BLOCK), F32)],
        compiler_params=_cparams(3),
        name="latent_attention",
    )(qz, qz, cn, mask, wuk_t, wuv_h, bias_tile)


def _t5_bucket_table(n):
    max_exact = REL_BUCKETS // 2
    d = np.arange(n)
    df = np.maximum(d, 1).astype(np.float64)
    large = max_exact + (np.log(df / max_exact) / math.log(REL_MAX_DIST / max_exact)
                         * (REL_BUCKETS - max_exact)).astype(np.int64)
    large = np.minimum(large, REL_BUCKETS - 1)
    return np.where(d < max_exact, d, large).astype(np.int32)


def _bias_tile(rel_bias):
    width = NEAR_SUB * Q_BLOCK
    table = _t5_bucket_table(width)
    assert (table[Q_BLOCK:] == REL_BUCKETS - 1).all()
    t = np.arange(Q_BLOCK)[:, None]
    s = np.arange(width)[None, :]
    dist = np.maximum((NEAR_SUB - 1) * Q_BLOCK + t - s, 0)
    idx = table[dist]
    tile = rel_bias[idx] - rel_bias[REL_BUCKETS - 1]
    return jnp.transpose(tile, (2, 0, 1)).astype(F32)


def _pool_kernel(u_ref, z_ref, w_ref, b_ref, sc_ref, y_ref, *, tr):
    g = pl.program_id(1)
    seq = u_ref.shape[1]
    pg = u_ref.shape[2]
    wlen = jnp.left_shift(jnp.int32(POOL_WINDOWS[0]), g)

    def do_tile(ext, r0):
        s2 = ext + pltpu.roll(ext, 1, 0)
        s4 = s2 + pltpu.roll(s2, 2, 0)
        s8 = s4 + pltpu.roll(s4, 4, 0)
        s16 = s8 + pltpu.roll(s8, 8, 0)
        win = jnp.where(g == 0, s2, jnp.where(g == 1, s4, jnp.where(g == 2, s8, s16)))
        win = win[POOL_HALO:]
        cur = ext[POOL_HALO:]
        t = r0 + lax.broadcasted_iota(jnp.int32, (tr, 1), 0)
        cnt = jnp.minimum(t + 1, wlen).astype(F32)
        pooled = win / cnt - cur
        mixed = jnp.dot(pooled.astype(BF16), w_ref[0], preferred_element_type=F32) + b_ref[0]
        mixed = mixed * sc_ref[0]
        zz = z_ref[0, pl.ds(r0, tr), :].astype(F32)
        y_ref[0, pl.ds(r0, tr), :] = (mixed * _silu(zz)).astype(BF16)

    first = jnp.concatenate([jnp.zeros((POOL_HALO, pg), F32), u_ref[0, 0:tr, :]], axis=0)
    do_tile(first, 0)

    def body(r, carry):
        r0 = pl.multiple_of(r * tr, tr)
        ext = u_ref[0, pl.ds(pl.multiple_of(r0 - POOL_HALO, 8), tr + POOL_HALO), :]
        do_tile(ext, r0)
        return carry

    lax.fori_loop(1, seq // tr, body, 0)


def _pool_mix(u, z, col0, w_grp, b_grp, scale, tr):
    b, seq, d_inner = u.shape
    n_groups, pg, _ = w_grp.shape
    zoff = col0 // pg
    return pl.pallas_call(
        functools.partial(_pool_kernel, tr=tr),
        grid=(b, n_groups),
        in_specs=[pl.BlockSpec((1, seq, pg), lambda bb, g: (bb, 0, g)),
                  pl.BlockSpec((1, seq, pg), lambda bb, g: (bb, 0, zoff + g)),
                  pl.BlockSpec((1, pg, pg), lambda bb, g: (g, 0, 0)),
                  pl.BlockSpec((1, 1, pg), lambda bb, g: (g, 0, 0)),
                  pl.BlockSpec((1, 1, pg), lambda bb, g: (g, 0, 0))],
        out_specs=pl.BlockSpec((1, seq, pg), lambda bb, g: (bb, 0, g)),
        out_shape=jax.ShapeDtypeStruct((b, seq, d_inner), BF16),
        compiler_params=_cparams(2),
        name="pool_mix",
    )(u, z, w_grp, b_grp.reshape(n_groups, 1, pg), scale.reshape(n_groups, 1, pg))


def _dsa_layer(h, norm_g, w_in, kv_norm, kidx_norm, w_uk, w_uv, w_out, rel_bias):
    b, seq, d_model = h.shape
    d_inner = w_out.shape[0]
    kv_lora, n_heads, head_dim = w_uk.shape
    idx_dim = kidx_norm.shape[0]
    a_in = w_in.shape[1]
    idx_heads = (a_in - 2 * d_inner - kv_lora - idx_dim) // (idx_dim + 1)
    o_ckv = d_inner
    o_qi = o_ckv + kv_lora
    o_ki = o_qi + idx_heads * idx_dim
    o_wi = o_ki + idx_dim
    o_z = o_wi + idx_heads
    assert o_z + d_inner == a_in and head_dim == Q_BLOCK and n_heads * head_dim == d_inner
    top_k = min(TOP_K_MAX, seq // 4)
    m = b * seq
    hm = h.reshape(m, d_model)

    w_qz = jnp.concatenate([w_in[:, :d_inner], w_in[:, o_z:]], axis=1).astype(BF16)
    w_qi = w_in[:, o_qi:o_ki].astype(BF16)
    n_small = kv_lora + idx_dim + idx_heads
    pad = (-n_small) % Q_BLOCK
    w_sm = jnp.concatenate([w_in[:, o_ckv:o_qi], w_in[:, o_ki:o_wi], w_in[:, o_wi:o_z],
                            jnp.zeros((d_model, pad), w_in.dtype)], axis=1).astype(BF16)

    tm = min(1024, m)
    qz = _norm_matmul(hm, norm_g, w_qz, BF16, tm, 512, "in_proj_a_qz")
    qi = _norm_matmul(hm, norm_g, w_qi, F32, tm, w_qi.shape[1] // 2, "in_proj_a_qidx")
    small = _norm_matmul(hm, norm_g, w_sm, F32, tm, w_sm.shape[1], "in_proj_a_small")

    cn, kn = _latent_prep(small, kv_norm, kidx_norm, min(1024, m))
    mask = _indexer(qi.reshape(b, seq, -1), small.reshape(b, seq, -1), kn.reshape(b, seq, idx_dim),
                    top_k, idx_heads, idx_dim, kv_lora + idx_dim)

    wuk_t = jnp.transpose(w_uk, (1, 2, 0)).astype(BF16)
    wuv_h = jnp.transpose(w_uv, (1, 0, 2)).astype(BF16)
    n_grp = min(16, n_heads)
    y = _attention(qz.reshape(b, seq, 2 * d_inner), cn.reshape(b, seq, kv_lora), mask,
                   wuk_t, wuv_h, _bias_tile(rel_bias), n_grp)
    out = _matmul_res(y.reshape(m, d_inner), w_out.astype(BF16), hm, min(512, m), 1024,
                      "out_proj_a")
    return out.reshape(b, seq, d_model)


def _pool_layer(h, norm_g, w_in, w_grp, b_grp, scale, w_out):
    b, seq, d_model = h.shape
    d_inner = w_out.shape[0]
    assert w_grp.shape[0] == len(POOL_WINDOWS) and POOL_WINDOWS == (2, 4, 8, 16)
    m = b * seq
    hm = h.reshape(m, d_model)
    tm = min(1024, m)
    u = _norm_matmul(hm, norm_g, w_in[:, :d_inner].astype(BF16), F32, tm, 512, "in_proj_b_u")
    z = _norm_matmul(hm, norm_g, w_in[:, d_inner:].astype(BF16), BF16, tm, 512, "in_proj_b_z")
    y = _pool_mix(u.reshape(b, seq, d_inner), z.reshape(b, seq, d_inner), 0,
                  w_grp.astype(BF16), b_grp, scale, min(256, seq))
    out = _matmul_res(y.reshape(m, d_inner), w_out.astype(BF16), hm, min(512, m), 1024,
                      "out_proj_b")
    return out.reshape(b, seq, d_model)


def kernel(x, norm_a, w_in_a, kv_norm_a, kidx_norm_a, w_uk_a, w_uv_a, w_out_a, norm_b, w_in_b,
           w_grp_b, b_grp_b, scale_b, w_out_b, rel_bias, final_norm):
    n_a = norm_a.shape[0]
    n_b = norm_b.shape[0]
    h = x
    for layer in range(n_a + n_b):
        j = layer // 2
        if layer % 2 == 0:
            h = _dsa_layer(h, norm_a[j], w_in_a[j], kv_norm_a[j], kidx_norm_a[j], w_uk_a[j],
                           w_uv_a[j], w_out_a[j], rel_bias)
        else:
            h = _pool_layer(h, norm_b[j], w_in_b[j], w_grp_b[j], b_grp_b[j], scale_b[j],
                            w_out_b[j])
    b, seq, d_model = h.shape
    out = _rmsnorm(h.reshape(b * seq, d_model), final_norm, min(1024, b * seq), "final_norm")
    return out.reshape(b, seq, d_model)
```

```python
import functools
import math

import numpy as np
import jax
import jax.numpy as jnp
from jax import lax
from jax.experimental import pallas as pl
from jax.experimental.pallas import tpu as pltpu

F32 = jnp.float32
BF16 = jnp.bfloat16

EPS = 1e-6
Q_BLOCK = 128
TOP_K_MAX = 256
REL_BUCKETS = 32
REL_MAX_DIST = 128
POOL_WINDOWS = (2, 4, 8, 16)
POOL_HALO = 16
NEG = -1e30
INT_MIN = -2 ** 31
LOG2E = math.log2(math.e)
INDEXER_ROWS = 512
NEAR_SUB = 2
VMEM_LIMIT = 56 * 1024 * 1024


def _cparams(n_grid):
    return pltpu.CompilerParams(dimension_semantics=("arbitrary",) * n_grid,
                                vmem_limit_bytes=VMEM_LIMIT)


def _silu(z):
    return z * (1.0 / (1.0 + jnp.exp(-z)))


def _norm_matmul_kernel(x_ref, g_ref, w_ref, o_ref, xn_ref):
    @pl.when(pl.program_id(1) == 0)
    def _():
        x = x_ref[...]
        ms = jnp.mean(x * x, axis=-1, keepdims=True)
        xn_ref[...] = (x * lax.rsqrt(ms + EPS) * g_ref[...]).astype(BF16)

    o_ref[...] = jnp.dot(xn_ref[...], w_ref[...],
                         preferred_element_type=F32).astype(o_ref.dtype)


def _norm_matmul(x, g, w, out_dtype, tm, tn, name):
    m, k = x.shape
    n = w.shape[1]
    tn = min(tn, n)
    return pl.pallas_call(
        _norm_matmul_kernel,
        grid=(m // tm, n // tn),
        in_specs=[pl.BlockSpec((tm, k), lambda i, j: (i, 0)),
                  pl.BlockSpec((1, k), lambda i, j: (0, 0)),
                  pl.BlockSpec((k, tn), lambda i, j: (0, j))],
        out_specs=pl.BlockSpec((tm, tn), lambda i, j: (i, j)),
        out_shape=jax.ShapeDtypeStruct((m, n), out_dtype),
        scratch_shapes=[pltpu.VMEM((tm, k), BF16)],
        compiler_params=_cparams(2),
        name=name,
    )(x, g.reshape(1, k), w)


def _matmul_res_kernel(y_ref, w_ref, r_ref, o_ref):
    o_ref[...] = jnp.dot(y_ref[...], w_ref[...], preferred_element_type=F32) + r_ref[...]


def _matmul_res(y, w, r, tm, tn, name):
    m, k = y.shape
    n = w.shape[1]
    tn = min(tn, n)
    return pl.pallas_call(
        _matmul_res_kernel,
        grid=(n // tn, m // tm),
        in_specs=[pl.BlockSpec((tm, k), lambda j, i: (i, 0)),
                  pl.BlockSpec((k, tn), lambda j, i: (0, j)),
                  pl.BlockSpec((tm, tn), lambda j, i: (i, j))],
        out_specs=pl.BlockSpec((tm, tn), lambda j, i: (i, j)),
        out_shape=jax.ShapeDtypeStruct((m, n), F32),
        compiler_params=_cparams(2),
        name=name,
    )(y, w, r)


def _matmul_res_norm_kernel(y_ref, w_ref, r_ref, g_ref, o_ref):
    h = jnp.dot(y_ref[...], w_ref[...], preferred_element_type=F32) + r_ref[...]
    ms = jnp.mean(h * h, axis=-1, keepdims=True)
    o_ref[...] = h * lax.rsqrt(ms + EPS) * g_ref[...]


def _matmul_res_norm(y, w, r, g, tm, name):
    m, k = y.shape
    n = w.shape[1]
    return pl.pallas_call(
        _matmul_res_norm_kernel,
        grid=(m // tm,),
        in_specs=[pl.BlockSpec((tm, k), lambda i: (i, 0)),
                  pl.BlockSpec((k, n), lambda i: (0, 0), pipeline_mode=pl.Buffered(1)),
                  pl.BlockSpec((tm, n), lambda i: (i, 0)),
                  pl.BlockSpec((1, n), lambda i: (0, 0))],
        out_specs=pl.BlockSpec((tm, n), lambda i: (i, 0)),
        out_shape=jax.ShapeDtypeStruct((m, n), F32),
        compiler_params=_cparams(1),
        name=name,
    )(y, w, r, g.reshape(1, n))


def _rmsnorm_kernel(x_ref, g_ref, o_ref):
    x = x_ref[...]
    ms = jnp.mean(x * x, axis=-1, keepdims=True)
    o_ref[...] = x * lax.rsqrt(ms + EPS) * g_ref[...]


def _rmsnorm(x, g, tm, name):
    m, k = x.shape
    return pl.pallas_call(
        _rmsnorm_kernel,
        grid=(m // tm,),
        in_specs=[pl.BlockSpec((tm, k), lambda i: (i, 0)),
                  pl.BlockSpec((1, k), lambda i: (0, 0))],
        out_specs=pl.BlockSpec((tm, k), lambda i: (i, 0)),
        out_shape=jax.ShapeDtypeStruct((m, k), F32),
        compiler_params=_cparams(1),
        name=name,
    )(x, g.reshape(1, k))


def _latent_kernel(sm_ref, kvn_ref, kin_ref, cn_ref, kn_ref, *, kv_lora, idx_dim):
    x = sm_ref[...]
    c = x[:, :kv_lora]
    k = x[:, kv_lora:kv_lora + idx_dim]
    cn = c * lax.rsqrt(jnp.mean(c * c, axis=-1, keepdims=True) + EPS) * kvn_ref[...]
    kn = k * lax.rsqrt(jnp.mean(k * k, axis=-1, keepdims=True) + EPS) * kin_ref[...]
    cn_ref[...] = cn.astype(BF16)
    kn_ref[...] = kn.astype(BF16)


def _latent_prep(small, kv_norm, kidx_norm, tm):
    m, ws = small.shape
    kv_lora = kv_norm.shape[0]
    idx_dim = kidx_norm.shape[0]
    return pl.pallas_call(
        functools.partial(_latent_kernel, kv_lora=kv_lora, idx_dim=idx_dim),
        grid=(m // tm,),
        in_specs=[pl.BlockSpec((tm, ws), lambda i: (i, 0)),
                  pl.BlockSpec((1, kv_lora), lambda i: (0, 0)),
                  pl.BlockSpec((1, idx_dim), lambda i: (0, 0))],
        out_specs=[pl.BlockSpec((tm, kv_lora), lambda i: (i, 0)),
                   pl.BlockSpec((tm, idx_dim), lambda i: (i, 0))],
        out_shape=[jax.ShapeDtypeStruct((m, kv_lora), BF16),
                   jax.ShapeDtypeStruct((m, idx_dim), BF16)],
        compiler_params=_cparams(1),
        name="latent_prep",
    )(small, kv_norm.reshape(1, kv_lora), kidx_norm.reshape(1, idx_dim))


def _indexer_kernel(qi_ref, sm_ref, kn_ref, mask_ref, *, span, **kw):
    i = pl.program_id(1)
    rows = qi_ref.shape[1]
    total = kn_ref.shape[1]
    for v in range(1, total // span + 1):
        @pl.when(((i + 1) * rows + span - 1) // span == v)
        def _(v=v):
            _indexer_extent(qi_ref, sm_ref, kn_ref, mask_ref, v * span, **kw)


def _indexer_extent(qi_ref, sm_ref, kn_ref, mask_ref, seq, *, top_k, n_heads, idx_dim, w_off):
    i = pl.program_id(1)
    rows = qi_ref.shape[1]
    n_sub = seq // Q_BLOCK
    q = qi_ref[0]
    w = sm_ref[0][:, w_off:w_off + n_heads] * (n_heads ** -0.5 * idx_dim ** -0.5)
    kn = kn_ref[0, :seq, :]

    score = jnp.zeros((rows, seq), F32)
    for h in range(n_heads):
        qh = q[:, h * idx_dim:(h + 1) * idx_dim].astype(BF16)
        s = lax.dot_general(qh, kn, (((1,), (1,)), ((), ())), preferred_element_type=F32)
        score = score + w[:, h:h + 1] * jnp.maximum(s, 0.0)

    bits = pltpu.bitcast(score + 0.0, jnp.int32)
    key = jnp.where(bits >= 0, bits, bits ^ jnp.int32(0x7FFFFFFF))
    t_pos = i * rows + lax.broadcasted_iota(jnp.int32, (rows, seq), 0)
    s_pos = lax.broadcasted_iota(jnp.int32, (rows, seq), 1)
    causal = s_pos <= t_pos
    key = jnp.where(causal, key, jnp.int32(INT_MIN))

    def count_ge(thr):
        return jnp.sum(jnp.where(key >= thr, 1.0, 0.0), axis=-1, keepdims=True)

    kf = float(top_k)
    thr = jnp.where(count_ge(jnp.zeros((rows, 1), jnp.int32)) >= kf,
                    jnp.int32(0), jnp.int32(INT_MIN))

    def bit_step(b, thr):
        cand = thr + jnp.left_shift(jnp.int32(1), 30 - b)
        return jnp.where(count_ge(cand) >= kf, cand, thr)

    thr = lax.fori_loop(0, 31, bit_step, thr)

    ge = key >= thr
    sel = jnp.where(causal, jnp.where(ge, 1.0, 0.0), 0.0)
    def write_mask(j, cols):
        for r in range(rows // Q_BLOCK):
            mask_ref[0, r, j] = cols[r * Q_BLOCK:(r + 1) * Q_BLOCK]

    for j in range(n_sub):
        blk = sel[:, j * Q_BLOCK:(j + 1) * Q_BLOCK]
        write_mask(j, jnp.where(blk > 0.5, 0.0, NEG).astype(BF16))
    for j in range(n_sub, mask_ref.shape[2]):
        write_mask(j, jnp.full((rows, Q_BLOCK), NEG, BF16))

    n_ge = jnp.sum(jnp.where(ge, 1.0, 0.0), axis=-1, keepdims=True)
    has_tie = jnp.logical_and(n_ge > kf, thr > jnp.int32(INT_MIN))
    any_tie = jnp.max(jnp.where(has_tie, 1.0, 0.0))

    @pl.when(any_tie > 0.0)
    def _():
        cw = 2 * Q_BLOCK
        gt = key > thr
        eq = key == thr
        need = kf - jnp.sum(jnp.where(gt, 1.0, 0.0), axis=-1, keepdims=True)
        r_id = lax.broadcasted_iota(jnp.int32, (cw, cw), 0)
        c_id = lax.broadcasted_iota(jnp.int32, (cw, cw), 1)
        upper = jnp.where(r_id < c_id, 1.0, 0.0).astype(BF16)
        carry = jnp.zeros((rows, 1), F32)
        for c in range(seq // cw):
            sl = slice(c * cw, (c + 1) * cw)
            eqc = jnp.where(eq[:, sl], 1.0, 0.0)
            rank = carry + jnp.dot(eqc.astype(BF16), upper, preferred_element_type=F32)
            keep = jnp.where(gt[:, sl], 1.0, jnp.where(rank < need, eqc, 0.0))
            keep = jnp.where(has_tie, keep, sel[:, sl])
            carry = carry + jnp.sum(eqc, axis=-1, keepdims=True)
            add = jnp.where(keep > 0.5, 0.0, NEG).astype(BF16)
            write_mask(2 * c, add[:, :Q_BLOCK])
            write_mask(2 * c + 1, add[:, Q_BLOCK:])


def _indexer(qi, small, kn, top_k, n_heads, idx_dim, w_off):
    b, seq, _ = qi.shape
    n_blk = seq // Q_BLOCK
    rows = min(INDEXER_ROWS, seq)
    return pl.pallas_call(
        functools.partial(_indexer_kernel, span=rows, top_k=top_k,
                          n_heads=n_heads, idx_dim=idx_dim, w_off=w_off),
        grid=(b, seq // rows),
        in_specs=[pl.BlockSpec((1, rows, qi.shape[2]), lambda bb, i: (bb, i, 0)),
                  pl.BlockSpec((1, rows, small.shape[2]), lambda bb, i: (bb, i, 0)),
                  pl.BlockSpec((1, seq, idx_dim), lambda bb, i: (bb, 0, 0))],
        out_specs=pl.BlockSpec((1, rows // Q_BLOCK, n_blk, Q_BLOCK, Q_BLOCK),
                               lambda bb, i: (bb, i, 0, 0, 0)),
        out_shape=jax.ShapeDtypeStruct((b, n_blk, n_blk, Q_BLOCK, Q_BLOCK), BF16),
        compiler_params=_cparams(2),
        name="indexer_mask",
    )(qi, small, kn)


def _attn_kernel(q_ref, z_ref, cn_ref, mask_ref, wuk_ref, wuv_ref, bias_ref, y_ref,
                 qlat_ref, s_ref, acc_ref, m_ref, l_ref, *, n_grp, head_dim, scale):
    i = pl.program_id(2)
    rows = n_grp * Q_BLOCK

    for h in range(n_grp):
        qh = q_ref[0, :, h * head_dim:(h + 1) * head_dim]
        ql = jnp.dot(qh, wuk_ref[h], preferred_element_type=F32) * scale
        qlat_ref[h * Q_BLOCK:(h + 1) * Q_BLOCK, :] = ql.astype(BF16)
    m_ref[...] = jnp.full(m_ref.shape, NEG, F32)

    def score_chunk(sub0, n_sb, bias_lo):
        width = n_sb * Q_BLOCK
        kstart = pl.multiple_of(sub0 * Q_BLOCK, Q_BLOCK)
        kc = cn_ref[0, pl.ds(kstart, width), :]
        s = lax.dot_general(qlat_ref[...], kc, (((1,), (1,)), ((), ())),
                            preferred_element_type=F32)
        mk = jnp.concatenate([mask_ref[0, 0, sub0 + j] for j in range(n_sb)],
                             axis=-1).astype(F32)
        s3 = s.reshape(n_grp, Q_BLOCK, width) + mk[None]
        if bias_lo is not None:
            s3 = s3 + bias_ref[:, :, bias_lo:bias_lo + width]
        s = s3.reshape(rows, width)
        mx = m_ref[...]
        for j in range(n_sb):
            blk = s[:, j * Q_BLOCK:(j + 1) * Q_BLOCK]
            s_ref[sub0 + j] = blk
            mx = jnp.maximum(mx, blk)
        m_ref[...] = mx

    def for_sub_blocks(count, step):
        def quad(j, carry):
            step(4 * j, 4)
            return carry

        lax.fori_loop(0, count // 4, quad, 0)

        @pl.when(count % 4 >= 2)
        def _():
            step((count // 4) * 4, 2)

        @pl.when(count % 2 == 1)
        def _():
            step(count - 1, 1)

    for_sub_blocks(jnp.maximum(i - 1, 0), lambda sub0, n_sb: score_chunk(sub0, n_sb, None))

    @pl.when(i == 0)
    def _():
        score_chunk(0, 1, Q_BLOCK)

    @pl.when(i > 0)
    def _():
        score_chunk(i - 1, 2, 0)

    m_ref[...] = jnp.broadcast_to(jnp.max(m_ref[...], axis=-1, keepdims=True), m_ref.shape)
    l_ref[...] = jnp.zeros(l_ref.shape, F32)
    acc_ref[...] = jnp.zeros(acc_ref.shape, F32)

    def pv_chunk(sub0, n_sb):
        kstart = pl.multiple_of(sub0 * Q_BLOCK, Q_BLOCK)
        kc = cn_ref[0, pl.ds(kstart, n_sb * Q_BLOCK), :]
        m = m_ref[...]
        ps = [jnp.exp2(s_ref[sub0 + j] - m) for j in range(n_sb)]
        l_ref[...] += functools.reduce(lambda a, b: a + b, ps)
        p = jnp.concatenate([pj.astype(BF16) for pj in ps], axis=-1)
        acc_ref[...] += jnp.dot(p, kc, preferred_element_type=F32)

    for_sub_blocks(i + 1, pv_chunk)

    o_lat = acc_ref[...] / jnp.sum(l_ref[...], axis=-1, keepdims=True)
    for h in range(n_grp):
        oh = jnp.dot(o_lat[h * Q_BLOCK:(h + 1) * Q_BLOCK].astype(BF16), wuv_ref[h],
                     preferred_element_type=F32)
        zz = z_ref[0, :, h * head_dim:(h + 1) * head_dim].astype(F32)
        y_ref[0, :, h * head_dim:(h + 1) * head_dim] = (oh * _silu(zz)).astype(BF16)


def _attention(qz, cn, mask, wuk_t, wuv_h, bias_tile, n_grp):
    b, seq, two_inner = qz.shape
    d_inner = two_inner // 2
    n_heads, head_dim, kv_lora = wuk_t.shape
    n_blk = seq // Q_BLOCK
    n_g = n_heads // n_grp
    gw = n_grp * head_dim
    rows = n_grp * Q_BLOCK
    return pl.pallas_call(
        functools.partial(_attn_kernel, n_grp=n_grp, head_dim=head_dim,
                          scale=head_dim ** -0.5 * LOG2E),
        grid=(n_g, b, n_blk),
        in_specs=[pl.BlockSpec((1, Q_BLOCK, gw), lambda g, bb, i: (bb, i, g)),
                  pl.BlockSpec((1, Q_BLOCK, gw), lambda g, bb, i: (bb, i, n_g + g)),
                  pl.BlockSpec((1, seq, kv_lora), lambda g, bb, i: (bb, 0, 0)),
                  pl.BlockSpec((1, 1, n_blk, Q_BLOCK, Q_BLOCK), lambda g, bb, i: (bb, i, 0, 0, 0)),
                  pl.BlockSpec((n_grp, head_dim, kv_lora), lambda g, bb, i: (g, 0, 0)),
                  pl.BlockSpec((n_grp, kv_lora, head_dim), lambda g, bb, i: (g, 0, 0)),
                  pl.BlockSpec((n_grp, Q_BLOCK, NEAR_SUB * Q_BLOCK), lambda g, bb, i: (g, 0, 0))],
        out_specs=pl.BlockSpec((1, Q_BLOCK, gw), lambda g, bb, i: (bb, i, g)),
        out_shape=jax.ShapeDtypeStruct((b, seq, d_inner), BF16),
        scratch_shapes=[pltpu.VMEM((rows, kv_lora), BF16),
                        pltpu.VMEM((n_blk, rows, Q_BLOCK), F32),
                        pltpu.VMEM((rows, kv_lora), F32),
                        pltpu.VMEM((rows, Q_BLOCK), F32),
                        pltpu.VMEM((rows, Q_BLOCK), F32)],
        compiler_params=_cparams(3),
        name="latent_attention",
    )(qz, qz, cn, mask, wuk_t, wuv_h, bias_tile * LOG2E)


def _t5_bucket_table(n):
    max_exact = REL_BUCKETS // 2
    d = np.arange(n)
    df = np.maximum(d, 1).astype(np.float64)
    large = max_exact + (np.log(df / max_exact) / math.log(REL_MAX_DIST / max_exact)
                         * (REL_BUCKETS - max_exact)).astype(np.int64)
    large = np.minimum(large, REL_BUCKETS - 1)
    return np.where(d < max_exact, d, large).astype(np.int32)


def _bias_tile(rel_bias):
    width = NEAR_SUB * Q_BLOCK
    period = width + Q_BLOCK
    table = _t5_bucket_table(period)
    assert (table[Q_BLOCK:] == REL_BUCKETS - 1).all()
    k = np.arange(period)
    dist = np.where(k <= width, (NEAR_SUB - 1) * Q_BLOCK - k, (NEAR_SUB - 1) * Q_BLOCK + period - k)
    w = rel_bias[table[np.maximum(dist, 0)]] - rel_bias[REL_BUCKETS - 1]
    flat = jnp.tile(w.T, (1, Q_BLOCK))[:, :Q_BLOCK * (period - 1)]
    return flat.reshape(-1, Q_BLOCK, period - 1)[:, :, :width].astype(F32)


def _pool_kernel(u_ref, z_ref, w_ref, b_ref, sc_ref, y_ref, *, tr):
    g = pl.program_id(1)
    seq = u_ref.shape[1]
    pg = u_ref.shape[2]
    wlen = jnp.left_shift(jnp.int32(POOL_WINDOWS[0]), g)

    def do_tile(ext, r0):
        s2 = ext + pltpu.roll(ext, 1, 0)
        s4 = s2 + pltpu.roll(s2, 2, 0)
        s8 = s4 + pltpu.roll(s4, 4, 0)
        s16 = s8 + pltpu.roll(s8, 8, 0)
        win = jnp.where(g == 0, s2, jnp.where(g == 1, s4, jnp.where(g == 2, s8, s16)))
        win = win[POOL_HALO:]
        cur = ext[POOL_HALO:]
        t = r0 + lax.broadcasted_iota(jnp.int32, (tr, 1), 0)
        cnt = jnp.minimum(t + 1, wlen).astype(F32)
        pooled = win / cnt - cur
        mixed = jnp.dot(pooled.astype(BF16), w_ref[0], preferred_element_type=F32) + b_ref[0]
        mixed = mixed * sc_ref[0]
        zz = z_ref[0, pl.ds(r0, tr), :].astype(F32)
        y_ref[0, pl.ds(r0, tr), :] = (mixed * _silu(zz)).astype(BF16)

    first = jnp.concatenate([jnp.zeros((POOL_HALO, pg), F32), u_ref[0, 0:tr, :].astype(F32)],
                            axis=0)
    do_tile(first, 0)

    def body(r, carry):
        r0 = pl.multiple_of(r * tr, tr)
        ext = u_ref[0, pl.ds(pl.multiple_of(r0 - POOL_HALO, POOL_HALO), tr + POOL_HALO), :]
        do_tile(ext.astype(F32), r0)
        return carry

    lax.fori_loop(1, seq // tr, body, 0)


def _pool_mix(u, z, col0, w_grp, b_grp, scale, tr):
    b, seq, _ = u.shape
    n_groups, pg, _ = w_grp.shape
    d_inner = n_groups * pg
    zoff = col0 // pg
    return pl.pallas_call(
        functools.partial(_pool_kernel, tr=tr),
        grid=(b, n_groups),
        in_specs=[pl.BlockSpec((1, seq, pg), lambda bb, g: (bb, 0, g)),
                  pl.BlockSpec((1, seq, pg), lambda bb, g: (bb, 0, zoff + g)),
                  pl.BlockSpec((1, pg, pg), lambda bb, g: (g, 0, 0)),
                  pl.BlockSpec((1, 1, pg), lambda bb, g: (g, 0, 0)),
                  pl.BlockSpec((1, 1, pg), lambda bb, g: (g, 0, 0))],
        out_specs=pl.BlockSpec((1, seq, pg), lambda bb, g: (bb, 0, g)),
        out_shape=jax.ShapeDtypeStruct((b, seq, d_inner), BF16),
        compiler_params=_cparams(2),
        name="pool_mix",
    )(u, z, w_grp, b_grp.reshape(n_groups, 1, pg), scale.reshape(n_groups, 1, pg))


def _dsa_layer(h, norm_g, w_in, kv_norm, kidx_norm, w_uk, w_uv, w_out, rel_bias):
    b, seq, d_model = h.shape
    d_inner = w_out.shape[0]
    kv_lora, n_heads, head_dim = w_uk.shape
    idx_dim = kidx_norm.shape[0]
    a_in = w_in.shape[1]
    idx_heads = (a_in - 2 * d_inner - kv_lora - idx_dim) // (idx_dim + 1)
    o_ckv = d_inner
    o_qi = o_ckv + kv_lora
    o_ki = o_qi + idx_heads * idx_dim
    o_wi = o_ki + idx_dim
    o_z = o_wi + idx_heads
    assert o_z + d_inner == a_in and head_dim == Q_BLOCK and n_heads * head_dim == d_inner
    top_k = min(TOP_K_MAX, seq // 4)
    m = b * seq
    hm = h.reshape(m, d_model)

    w_qz = jnp.concatenate([w_in[:, :d_inner], w_in[:, o_z:]], axis=1).astype(BF16)
    w_qi = w_in[:, o_qi:o_ki].astype(BF16)
    n_small = kv_lora + idx_dim + idx_heads
    pad = (-n_small) % Q_BLOCK
    w_sm = jnp.concatenate([w_in[:, o_ckv:o_qi], w_in[:, o_ki:o_wi], w_in[:, o_wi:o_z],
                            jnp.zeros((d_model, pad), w_in.dtype)], axis=1).astype(BF16)

    tm = min(1024, m)
    qz = _norm_matmul(hm, norm_g, w_qz, BF16, tm, 512, "in_proj_a_qz")
    qi = _norm_matmul(hm, norm_g, w_qi, F32, tm, w_qi.shape[1] // 2, "in_proj_a_qidx")
    small = _norm_matmul(hm, norm_g, w_sm, F32, tm, w_sm.shape[1], "in_proj_a_small")

    cn, kn = _latent_prep(small, kv_norm, kidx_norm, min(1024, m))
    mask = _indexer(qi.reshape(b, seq, -1), small.reshape(b, seq, -1), kn.reshape(b, seq, idx_dim),
                    top_k, idx_heads, idx_dim, kv_lora + idx_dim)

    wuk_t = jnp.transpose(w_uk, (1, 2, 0)).astype(BF16)
    wuv_h = jnp.transpose(w_uv, (1, 0, 2)).astype(BF16)
    n_grp = min(16, n_heads)
    y = _attention(qz.reshape(b, seq, 2 * d_inner), cn.reshape(b, seq, kv_lora), mask,
                   wuk_t, wuv_h, _bias_tile(rel_bias), n_grp)
    out = _matmul_res(y.reshape(m, d_inner), w_out.astype(BF16), hm, min(512, m), 1024,
                      "out_proj_a")
    return out.reshape(b, seq, d_model)


def _pool_layer(h, norm_g, w_in, w_grp, b_grp, scale, w_out, out_norm_g):
    b, seq, d_model = h.shape
    d_inner = w_out.shape[0]
    assert w_grp.shape[0] == len(POOL_WINDOWS) and POOL_WINDOWS == (2, 4, 8, 16)
    m = b * seq
    hm = h.reshape(m, d_model)
    uz = _norm_matmul(hm, norm_g, w_in.astype(BF16), BF16, min(1024, m), 512, "in_proj_b")
    uz = uz.reshape(b, seq, 2 * d_inner)
    y = _pool_mix(uz, uz, d_inner, w_grp.astype(BF16), b_grp, scale, min(256, seq))
    y = y.reshape(m, d_inner)
    if out_norm_g is None:
        out = _matmul_res(y, w_out.astype(BF16), hm, min(512, m), 1024, "out_proj_b")
    else:
        out = _matmul_res_norm(y, w_out.astype(BF16), hm, out_norm_g, min(512, m),
                               "out_proj_b_norm")
    return out.reshape(b, seq, d_model)


def kernel(x, norm_a, w_in_a, kv_norm_a, kidx_norm_a, w_uk_a, w_uv_a, w_out_a, norm_b, w_in_b,
           w_grp_b, b_grp_b, scale_b, w_out_b, rel_bias, final_norm):
    n_a = norm_a.shape[0]
    n_b = norm_b.shape[0]
    depth = n_a + n_b
    h = x
    for layer in range(depth):
        j = layer // 2
        if layer % 2 == 0:
            h = _dsa_layer(h, norm_a[j], w_in_a[j], kv_norm_a[j], kidx_norm_a[j], w_uk_a[j],
                           w_uv_a[j], w_out_a[j], rel_bias)
        else:
            h = _pool_layer(h, norm_b[j], w_in_b[j], w_grp_b[j], b_grp_b[j], scale_b[j],
                            w_out_b[j], final_norm if layer == depth - 1 else None)
    if depth % 2 == 0:
        return h
    b, seq, d_model = h.shape
    out = _rmsnorm(h.reshape(b * seq, d_model), final_norm, min(1024, b * seq), "final_norm")
    return out.reshape(b, seq, d_model)
```

```python
import functools
import math

import numpy as np
import jax
import jax.numpy as jnp
from jax import lax
from jax.experimental import pallas as pl
from jax.experimental.pallas import tpu as pltpu

F32 = jnp.float32
BF16 = jnp.bfloat16

EPS = 1e-6
Q_BLOCK = 128
TOP_K_MAX = 256
REL_BUCKETS = 32
REL_MAX_DIST = 128
POOL_WINDOWS = (2, 4, 8, 16)
POOL_HALO = 16
NEG = -1e30
INT_MIN = -2 ** 31
LOG2E = math.log2(math.e)
INDEXER_ROWS = 256
INDEXER_SPAN = 256
NEAR_SUB = 2
VMEM_LIMIT = 56 * 1024 * 1024


def _cparams(n_grid):
    return pltpu.CompilerParams(dimension_semantics=("arbitrary",) * n_grid,
                                vmem_limit_bytes=VMEM_LIMIT)


def _silu(z):
    return z * (1.0 / (1.0 + jnp.exp(-z)))


def _norm_matmul_kernel(x_ref, g_ref, w_ref, o_ref, xn_ref):
    @pl.when(pl.program_id(1) == 0)
    def _():
        x = x_ref[...]
        ms = jnp.mean(x * x, axis=-1, keepdims=True)
        xn_ref[...] = (x * lax.rsqrt(ms + EPS) * g_ref[...]).astype(BF16)

    o_ref[...] = jnp.dot(xn_ref[...], w_ref[...],
                         preferred_element_type=F32).astype(o_ref.dtype)


def _norm_matmul(x, g, w, out_dtype, tm, tn, name):
    m, k = x.shape
    n = w.shape[1]
    tn = min(tn, n)
    return pl.pallas_call(
        _norm_matmul_kernel,
        grid=(m // tm, n // tn),
        in_specs=[pl.BlockSpec((tm, k), lambda i, j: (i, 0)),
                  pl.BlockSpec((1, k), lambda i, j: (0, 0)),
                  pl.BlockSpec((k, tn), lambda i, j: (0, j))],
        out_specs=pl.BlockSpec((tm, tn), lambda i, j: (i, j)),
        out_shape=jax.ShapeDtypeStruct((m, n), out_dtype),
        scratch_shapes=[pltpu.VMEM((tm, k), BF16)],
        compiler_params=_cparams(2),
        name=name,
    )(x, g.reshape(1, k), w)


def _matmul_res_kernel(y_ref, w_ref, r_ref, o_ref):
    o_ref[...] = jnp.dot(y_ref[...], w_ref[...], preferred_element_type=F32) + r_ref[...]


def _matmul_res(y, w, r, tm, tn, name):
    m, k = y.shape
    n = w.shape[1]
    tn = min(tn, n)
    return pl.pallas_call(
        _matmul_res_kernel,
        grid=(n // tn, m // tm),
        in_specs=[pl.BlockSpec((tm, k), lambda j, i: (i, 0)),
                  pl.BlockSpec((k, tn), lambda j, i: (0, j)),
                  pl.BlockSpec((tm, tn), lambda j, i: (i, j))],
        out_specs=pl.BlockSpec((tm, tn), lambda j, i: (i, j)),
        out_shape=jax.ShapeDtypeStruct((m, n), F32),
        compiler_params=_cparams(2),
        name=name,
    )(y, w, r)


def _matmul_res_norm_kernel(y_ref, w_ref, r_ref, g_ref, o_ref):
    h = jnp.dot(y_ref[...], w_ref[...], preferred_element_type=F32) + r_ref[...]
    ms = jnp.mean(h * h, axis=-1, keepdims=True)
    o_ref[...] = h * lax.rsqrt(ms + EPS) * g_ref[...]


def _matmul_res_norm(y, w, r, g, tm, name):
    m, k = y.shape
    n = w.shape[1]
    return pl.pallas_call(
        _matmul_res_norm_kernel,
        grid=(m // tm,),
        in_specs=[pl.BlockSpec((tm, k), lambda i: (i, 0)),
                  pl.BlockSpec((k, n), lambda i: (0, 0), pipeline_mode=pl.Buffered(1)),
                  pl.BlockSpec((tm, n), lambda i: (i, 0)),
                  pl.BlockSpec((1, n), lambda i: (0, 0))],
        out_specs=pl.BlockSpec((tm, n), lambda i: (i, 0)),
        out_shape=jax.ShapeDtypeStruct((m, n), F32),
        compiler_params=_cparams(1),
        name=name,
    )(y, w, r, g.reshape(1, n))


def _rmsnorm_kernel(x_ref, g_ref, o_ref):
    x = x_ref[...]
    ms = jnp.mean(x * x, axis=-1, keepdims=True)
    o_ref[...] = x * lax.rsqrt(ms + EPS) * g_ref[...]


def _rmsnorm(x, g, tm, name):
    m, k = x.shape
    return pl.pallas_call(
        _rmsnorm_kernel,
        grid=(m // tm,),
        in_specs=[pl.BlockSpec((tm, k), lambda i: (i, 0)),
                  pl.BlockSpec((1, k), lambda i: (0, 0))],
        out_specs=pl.BlockSpec((tm, k), lambda i: (i, 0)),
        out_shape=jax.ShapeDtypeStruct((m, k), F32),
        compiler_params=_cparams(1),
        name=name,
    )(x, g.reshape(1, k))


def _idx_proj_kernel(x_ref, g_ref, w_ref, kvn_ref, kin_ref, qi_ref, cn_ref, kn_ref, kw_ref):
    n_qi = qi_ref.shape[1]
    kv_lora = cn_ref.shape[1]
    idx_dim = kn_ref.shape[1]
    x = x_ref[...]
    ms = jnp.mean(x * x, axis=-1, keepdims=True)
    xn = (x * lax.rsqrt(ms + EPS) * g_ref[...]).astype(BF16)
    p = jnp.dot(xn, w_ref[...], preferred_element_type=F32)
    qi_ref[...] = p[:, :n_qi]
    c = p[:, n_qi:n_qi + kv_lora]
    kw = p[:, n_qi + kv_lora:]
    k = kw[:, :idx_dim]
    cn = c * lax.rsqrt(jnp.mean(c * c, axis=-1, keepdims=True) + EPS) * kvn_ref[...]
    kn = k * lax.rsqrt(jnp.mean(k * k, axis=-1, keepdims=True) + EPS) * kin_ref[...]
    cn_ref[...] = cn.astype(BF16)
    kn_ref[...] = kn.astype(BF16)
    kw_ref[...] = kw


def _idx_proj(x, g, w, kv_norm, kidx_norm, n_qi, tm):
    m, k = x.shape
    n = w.shape[1]
    kv_lora = kv_norm.shape[0]
    idx_dim = kidx_norm.shape[0]
    n_kw = n - n_qi - kv_lora
    return pl.pallas_call(
        _idx_proj_kernel,
        grid=(m // tm,),
        in_specs=[pl.BlockSpec((tm, k), lambda i: (i, 0)),
                  pl.BlockSpec((1, k), lambda i: (0, 0)),
                  pl.BlockSpec((k, n), lambda i: (0, 0)),
                  pl.BlockSpec((1, kv_lora), lambda i: (0, 0)),
                  pl.BlockSpec((1, idx_dim), lambda i: (0, 0))],
        out_specs=[pl.BlockSpec((tm, n_qi), lambda i: (i, 0)),
                   pl.BlockSpec((tm, kv_lora), lambda i: (i, 0)),
                   pl.BlockSpec((tm, idx_dim), lambda i: (i, 0)),
                   pl.BlockSpec((tm, n_kw), lambda i: (i, 0))],
        out_shape=[jax.ShapeDtypeStruct((m, n_qi), F32),
                   jax.ShapeDtypeStruct((m, kv_lora), BF16),
                   jax.ShapeDtypeStruct((m, idx_dim), BF16),
                   jax.ShapeDtypeStruct((m, n_kw), F32)],
        compiler_params=_cparams(1),
        name="in_proj_a_idx",
    )(x, g.reshape(1, k), w, kv_norm.reshape(1, kv_lora), kidx_norm.reshape(1, idx_dim))


def _indexer_kernel(qi_ref, sm_ref, kn_ref, mask_ref, *, span, **kw):
    i = pl.program_id(1)
    rows = qi_ref.shape[1]
    total = kn_ref.shape[1]
    for v in range(1, total // span + 1):
        @pl.when(((i + 1) * rows + span - 1) // span == v)
        def _(v=v):
            _indexer_extent(qi_ref, sm_ref, kn_ref, mask_ref, v * span, **kw)


def _indexer_extent(qi_ref, sm_ref, kn_ref, mask_ref, seq, *, top_k, n_heads, idx_dim, w_off):
    i = pl.program_id(1)
    rows = qi_ref.shape[1]
    n_sub = seq // Q_BLOCK
    q = qi_ref[0]
    w = sm_ref[0][:, w_off:w_off + n_heads] * (n_heads ** -0.5 * idx_dim ** -0.5)
    kn = kn_ref[0, :seq, :]

    score = jnp.zeros((rows, seq), F32)
    for h in range(n_heads):
        qh = q[:, h * idx_dim:(h + 1) * idx_dim].astype(BF16)
        s = lax.dot_general(qh, kn, (((1,), (1,)), ((), ())), preferred_element_type=F32)
        score = score + w[:, h:h + 1] * jnp.maximum(s, 0.0)

    bits = pltpu.bitcast(score + 0.0, jnp.int32)
    key = jnp.where(bits >= 0, bits, bits ^ jnp.int32(0x7FFFFFFF))
    t_pos = i * rows + lax.broadcasted_iota(jnp.int32, (rows, seq), 0)
    s_pos = lax.broadcasted_iota(jnp.int32, (rows, seq), 1)
    causal = s_pos <= t_pos
    key = jnp.where(causal, key, jnp.int32(INT_MIN))

    def count_ge(thr):
        return jnp.sum(jnp.where(key >= thr, 1.0, 0.0), axis=-1, keepdims=True)

    kf = float(top_k)
    thr = jnp.where(count_ge(jnp.zeros((rows, 1), jnp.int32)) >= kf,
                    jnp.int32(0), jnp.int32(INT_MIN))

    def bit_step(b, thr):
        cand = thr + jnp.left_shift(jnp.int32(1), 30 - b)
        return jnp.where(count_ge(cand) >= kf, cand, thr)

    thr = lax.fori_loop(0, 31, bit_step, thr)

    ge = key >= thr
    sel = jnp.where(causal, jnp.where(ge, 1.0, 0.0), 0.0)
    def write_mask(j, cols):
        for r in range(rows // Q_BLOCK):
            mask_ref[0, r, j] = cols[r * Q_BLOCK:(r + 1) * Q_BLOCK]

    for j in range(n_sub):
        blk = sel[:, j * Q_BLOCK:(j + 1) * Q_BLOCK]
        write_mask(j, jnp.where(blk > 0.5, 0.0, NEG).astype(BF16))
    for j in range(n_sub, mask_ref.shape[2]):
        write_mask(j, jnp.full((rows, Q_BLOCK), NEG, BF16))

    n_ge = jnp.sum(jnp.where(ge, 1.0, 0.0), axis=-1, keepdims=True)
    has_tie = jnp.logical_and(n_ge > kf, thr > jnp.int32(INT_MIN))
    any_tie = jnp.max(jnp.where(has_tie, 1.0, 0.0))

    @pl.when(any_tie > 0.0)
    def _():
        cw = 2 * Q_BLOCK
        gt = key > thr
        eq = key == thr
        need = kf - jnp.sum(jnp.where(gt, 1.0, 0.0), axis=-1, keepdims=True)
        r_id = lax.broadcasted_iota(jnp.int32, (cw, cw), 0)
        c_id = lax.broadcasted_iota(jnp.int32, (cw, cw), 1)
        upper = jnp.where(r_id < c_id, 1.0, 0.0).astype(BF16)
        carry = jnp.zeros((rows, 1), F32)
        for c in range(seq // cw):
            sl = slice(c * cw, (c + 1) * cw)
            eqc = jnp.where(eq[:, sl], 1.0, 0.0)
            rank = carry + jnp.dot(eqc.astype(BF16), upper, preferred_element_type=F32)
            keep = jnp.where(gt[:, sl], 1.0, jnp.where(rank < need, eqc, 0.0))
            keep = jnp.where(has_tie, keep, sel[:, sl])
            carry = carry + jnp.sum(eqc, axis=-1, keepdims=True)
            add = jnp.where(keep > 0.5, 0.0, NEG).astype(BF16)
            write_mask(2 * c, add[:, :Q_BLOCK])
            write_mask(2 * c + 1, add[:, Q_BLOCK:])


def _indexer(qi, small, kn, top_k, n_heads, idx_dim, w_off):
    b, seq, _ = qi.shape
    n_blk = seq // Q_BLOCK
    rows = min(INDEXER_ROWS, seq)
    return pl.pallas_call(
        functools.partial(_indexer_kernel, span=min(INDEXER_SPAN, seq), top_k=top_k,
                          n_heads=n_heads, idx_dim=idx_dim, w_off=w_off),
        grid=(b, seq // rows),
        in_specs=[pl.BlockSpec((1, rows, qi.shape[2]), lambda bb, i: (bb, i, 0)),
                  pl.BlockSpec((1, rows, small.shape[2]), lambda bb, i: (bb, i, 0)),
                  pl.BlockSpec((1, seq, idx_dim), lambda bb, i: (bb, 0, 0))],
        out_specs=pl.BlockSpec((1, rows // Q_BLOCK, n_blk, Q_BLOCK, Q_BLOCK),
                               lambda bb, i: (bb, i, 0, 0, 0)),
        out_shape=jax.ShapeDtypeStruct((b, n_blk, n_blk, Q_BLOCK, Q_BLOCK), BF16),
        compiler_params=_cparams(2),
        name="indexer_mask",
    )(qi, small, kn)


def _attn_kernel(q_ref, z_ref, cn_ref, mask_ref, wuk_ref, wuv_ref, bias_ref, y_ref,
                 qlat_ref, s_ref, acc_ref, m_ref, l_ref, *, n_grp, head_dim, scale):
    i = pl.program_id(2)
    rows = n_grp * Q_BLOCK

    for h in range(n_grp):
        qh = q_ref[0, :, h * head_dim:(h + 1) * head_dim]
        ql = jnp.dot(qh, wuk_ref[h], preferred_element_type=F32) * scale
        qlat_ref[h * Q_BLOCK:(h + 1) * Q_BLOCK, :] = ql.astype(BF16)
    m_ref[...] = jnp.full(m_ref.shape, NEG, F32)

    def score_chunk(sub0, n_sb, n_bias=0):
        width = n_sb * Q_BLOCK
        kstart = pl.multiple_of(sub0 * Q_BLOCK, Q_BLOCK)
        kc = cn_ref[0, pl.ds(kstart, width), :]
        s = lax.dot_general(qlat_ref[...], kc, (((1,), (1,)), ((), ())),
                            preferred_element_type=F32)
        s3 = s.reshape(n_grp, Q_BLOCK, width)
        mx = m_ref[...]
        for j in range(n_sb):
            blk = s3[:, :, j * Q_BLOCK:(j + 1) * Q_BLOCK] + mask_ref[0, 0, sub0 + j].astype(F32)[None]
            jb = j - (n_sb - n_bias)
            if jb >= 0:
                lo = (NEAR_SUB - n_bias + jb) * Q_BLOCK
                blk = blk + bias_ref[:, :, lo:lo + Q_BLOCK]
            blk = blk.reshape(rows, Q_BLOCK)
            s_ref[sub0 + j] = blk
            mx = jnp.maximum(mx, blk)
        m_ref[...] = mx

    def quads_then_tail(n_far, chunk, finish):
        n_quad = n_far // 4

        def quad(j, carry):
            chunk(4 * j, 4, 0)
            return carry

        lax.fori_loop(0, n_quad, quad, 0)

        @pl.when(i == 0)
        def _():
            chunk(0, 1, 1)
            finish()

        for rem in range(4):
            @pl.when(jnp.logical_and(i > 0, n_far - 4 * n_quad == rem))
            def _(rem=rem):
                chunk(4 * n_quad, rem + NEAR_SUB, NEAR_SUB)
                finish()

    def finish_scores():
        m_ref[...] = jnp.broadcast_to(jnp.max(m_ref[...], axis=-1, keepdims=True), m_ref.shape)
        l_ref[...] = jnp.zeros(l_ref.shape, F32)
        acc_ref[...] = jnp.zeros(acc_ref.shape, F32)

    def pv_chunk(sub0, n_sb, n_bias=0):
        kstart = pl.multiple_of(sub0 * Q_BLOCK, Q_BLOCK)
        kc = cn_ref[0, pl.ds(kstart, n_sb * Q_BLOCK), :]
        m = m_ref[...]
        ps = [jnp.exp2(s_ref[sub0 + j] - m) for j in range(n_sb)]
        l_ref[...] += functools.reduce(lambda a, b: a + b, ps)
        p = jnp.concatenate([pj.astype(BF16) for pj in ps], axis=-1)
        acc_ref[...] += jnp.dot(p, kc, preferred_element_type=F32)

    def finish_output():
        l = jnp.sum(l_ref[...], axis=-1, keepdims=True)
        for h in range(n_grp):
            hs = slice(h * Q_BLOCK, (h + 1) * Q_BLOCK)
            oh = jnp.dot(acc_ref[hs, :].astype(BF16), wuv_ref[h],
                         preferred_element_type=F32) / l[hs]
            zz = z_ref[0, :, h * head_dim:(h + 1) * head_dim].astype(F32)
            y_ref[0, :, h * head_dim:(h + 1) * head_dim] = (oh * _silu(zz)).astype(BF16)

    n_far = jnp.maximum(i - 1, 0)
    quads_then_tail(n_far, score_chunk, finish_scores)
    quads_then_tail(n_far, pv_chunk, finish_output)


def _attention(qz, cn, mask, wuk_t, wuv_h, bias_tile, n_grp):
    b, seq, two_inner = qz.shape
    d_inner = two_inner // 2
    n_heads, head_dim, kv_lora = wuk_t.shape
    n_blk = seq // Q_BLOCK
    n_g = n_heads // n_grp
    gw = n_grp * head_dim
    rows = n_grp * Q_BLOCK
    return pl.pallas_call(
        functools.partial(_attn_kernel, n_grp=n_grp, head_dim=head_dim,
                          scale=head_dim ** -0.5 * LOG2E),
        grid=(n_g, b, n_blk),
        in_specs=[pl.BlockSpec((1, Q_BLOCK, gw), lambda g, bb, i: (bb, i, g)),
                  pl.BlockSpec((1, Q_BLOCK, gw), lambda g, bb, i: (bb, i, n_g + g)),
                  pl.BlockSpec((1, seq, kv_lora), lambda g, bb, i: (bb, 0, 0)),
                  pl.BlockSpec((1, 1, n_blk, Q_BLOCK, Q_BLOCK), lambda g, bb, i: (bb, i, 0, 0, 0)),
                  pl.BlockSpec((n_grp, head_dim, kv_lora), lambda g, bb, i: (g, 0, 0)),
                  pl.BlockSpec((n_grp, kv_lora, head_dim), lambda g, bb, i: (g, 0, 0)),
                  pl.BlockSpec((n_grp, Q_BLOCK, NEAR_SUB * Q_BLOCK), lambda g, bb, i: (g, 0, 0))],
        out_specs=pl.BlockSpec((1, Q_BLOCK, gw), lambda g, bb, i: (bb, i, g)),
        out_shape=jax.ShapeDtypeStruct((b, seq, d_inner), BF16),
        scratch_shapes=[pltpu.VMEM((rows, kv_lora), BF16),
                        pltpu.VMEM((n_blk, rows, Q_BLOCK), F32),
                        pltpu.VMEM((rows, kv_lora), F32),
                        pltpu.VMEM((rows, Q_BLOCK), F32),
                        pltpu.VMEM((rows, Q_BLOCK), F32)],
        compiler_params=_cparams(3),
        name="latent_attention",
    )(qz, qz, cn, mask, wuk_t, wuv_h, bias_tile * LOG2E)


def _t5_bucket_table(n):
    max_exact = REL_BUCKETS // 2
    d = np.arange(n)
    df = np.maximum(d, 1).astype(np.float64)
    large = max_exact + (np.log(df / max_exact) / math.log(REL_MAX_DIST / max_exact)
                         * (REL_BUCKETS - max_exact)).astype(np.int64)
    large = np.minimum(large, REL_BUCKETS - 1)
    return np.where(d < max_exact, d, large).astype(np.int32)


def _bias_tile(rel_bias):
    width = NEAR_SUB * Q_BLOCK
    period = width + Q_BLOCK
    table = _t5_bucket_table(period)
    assert (table[Q_BLOCK:] == REL_BUCKETS - 1).all()
    k = np.arange(period)
    dist = np.where(k <= width, (NEAR_SUB - 1) * Q_BLOCK - k, (NEAR_SUB - 1) * Q_BLOCK + period - k)
    w = rel_bias[table[np.maximum(dist, 0)]] - rel_bias[REL_BUCKETS - 1]
    flat = jnp.tile(w.T, (1, Q_BLOCK))[:, :Q_BLOCK * (period - 1)]
    return flat.reshape(-1, Q_BLOCK, period - 1)[:, :, :width].astype(F32)


def _pool_kernel(u_ref, z_ref, w_ref, b_ref, sc_ref, y_ref, *, tr):
    g = pl.program_id(1)
    for gi, window in enumerate(POOL_WINDOWS):
        @pl.when(g == gi)
        def _(window=window):
            _pool_group(u_ref, z_ref, w_ref, b_ref, sc_ref, y_ref, window, tr)


def _pool_group(u_ref, z_ref, w_ref, b_ref, sc_ref, y_ref, window, tr):
    seq = u_ref.shape[1]
    pg = u_ref.shape[2]

    def window_sum(ext):
        acc, span = ext, 1
        while span < window:
            acc = acc + pltpu.roll(acc, span, 0)
            span *= 2
        return acc[POOL_HALO:]

    def mix_and_gate(pooled, r0):
        mixed = jnp.dot(pooled.astype(BF16), w_ref[0], preferred_element_type=F32) + b_ref[0]
        mixed = mixed * sc_ref[0]
        zz = z_ref[0, pl.ds(r0, tr), :].astype(F32)
        y_ref[0, pl.ds(r0, tr), :] = (mixed * _silu(zz)).astype(BF16)

    first = jnp.concatenate([jnp.zeros((POOL_HALO, pg), F32), u_ref[0, 0:tr, :].astype(F32)],
                            axis=0)
    t = lax.broadcasted_iota(jnp.int32, (tr, 1), 0)
    cnt = jnp.minimum(t + 1, window).astype(F32)
    mix_and_gate(window_sum(first) / cnt - first[POOL_HALO:], 0)

    def body(r, carry):
        r0 = pl.multiple_of(r * tr, tr)
        ext = u_ref[0, pl.ds(pl.multiple_of(r0 - POOL_HALO, POOL_HALO), tr + POOL_HALO), :]
        ext = ext.astype(F32)
        mix_and_gate(window_sum(ext) * (1.0 / window) - ext[POOL_HALO:], r0)
        return carry

    lax.fori_loop(1, seq // tr, body, 0)


def _pool_mix(u, z, col0, w_grp, b_grp, scale, tr):
    b, seq, _ = u.shape
    n_groups, pg, _ = w_grp.shape
    d_inner = n_groups * pg
    zoff = col0 // pg
    return pl.pallas_call(
        functools.partial(_pool_kernel, tr=tr),
        grid=(b, n_groups),
        in_specs=[pl.BlockSpec((1, seq, pg), lambda bb, g: (bb, 0, g)),
                  pl.BlockSpec((1, seq, pg), lambda bb, g: (bb, 0, zoff + g)),
                  pl.BlockSpec((1, pg, pg), lambda bb, g: (g, 0, 0)),
                  pl.BlockSpec((1, 1, pg), lambda bb, g: (g, 0, 0)),
                  pl.BlockSpec((1, 1, pg), lambda bb, g: (g, 0, 0))],
        out_specs=pl.BlockSpec((1, seq, pg), lambda bb, g: (bb, 0, g)),
        out_shape=jax.ShapeDtypeStruct((b, seq, d_inner), BF16),
        compiler_params=_cparams(2),
        name="pool_mix",
    )(u, z, w_grp, b_grp.reshape(n_groups, 1, pg), scale.reshape(n_groups, 1, pg))


def _dsa_layer(h, norm_g, w_in, kv_norm, kidx_norm, w_uk, w_uv, w_out, rel_bias):
    b, seq, d_model = h.shape
    d_inner = w_out.shape[0]
    kv_lora, n_heads, head_dim = w_uk.shape
    idx_dim = kidx_norm.shape[0]
    a_in = w_in.shape[1]
    idx_heads = (a_in - 2 * d_inner - kv_lora - idx_dim) // (idx_dim + 1)
    o_ckv = d_inner
    o_qi = o_ckv + kv_lora
    o_ki = o_qi + idx_heads * idx_dim
    o_wi = o_ki + idx_dim
    o_z = o_wi + idx_heads
    assert o_z + d_inner == a_in and head_dim == Q_BLOCK and n_heads * head_dim == d_inner
    top_k = min(TOP_K_MAX, seq // 4)
    m = b * seq
    hm = h.reshape(m, d_model)

    w_qz = jnp.concatenate([w_in[:, :d_inner], w_in[:, o_z:]], axis=1).astype(BF16)
    n_qi = idx_heads * idx_dim
    pad = (-(idx_dim + idx_heads)) % Q_BLOCK
    w_idx = jnp.concatenate([w_in[:, o_qi:o_ki], w_in[:, o_ckv:o_qi], w_in[:, o_ki:o_z],
                             jnp.zeros((d_model, pad), w_in.dtype)], axis=1).astype(BF16)

    qz = _norm_matmul(hm, norm_g, w_qz, BF16, min(1024, m), 2048, "in_proj_a_qz")
    qi, cn, kn, kw = _idx_proj(hm, norm_g, w_idx, kv_norm, kidx_norm, n_qi, min(512, m))
    mask = _indexer(qi.reshape(b, seq, -1), kw.reshape(b, seq, -1), kn.reshape(b, seq, idx_dim),
                    top_k, idx_heads, idx_dim, idx_dim)

    wuk_t = jnp.transpose(w_uk, (1, 2, 0)).astype(BF16)
    wuv_h = jnp.transpose(w_uv, (1, 0, 2)).astype(BF16)
    n_grp = min(16, n_heads)
    y = _attention(qz.reshape(b, seq, 2 * d_inner), cn.reshape(b, seq, kv_lora), mask,
                   wuk_t, wuv_h, _bias_tile(rel_bias), n_grp)
    out = _matmul_res(y.reshape(m, d_inner), w_out.astype(BF16), hm, min(512, m), 1024,
                      "out_proj_a")
    return out.reshape(b, seq, d_model)


def _pool_layer(h, norm_g, w_in, w_grp, b_grp, scale, w_out, out_norm_g):
    b, seq, d_model = h.shape
    d_inner = w_out.shape[0]
    assert w_grp.shape[0] == len(POOL_WINDOWS) and POOL_WINDOWS == (2, 4, 8, 16)
    m = b * seq
    hm = h.reshape(m, d_model)
    uz = _norm_matmul(hm, norm_g, w_in.astype(BF16), BF16, min(1024, m), 2048, "in_proj_b")
    uz = uz.reshape(b, seq, 2 * d_inner)
    y = _pool_mix(uz, uz, d_inner, w_grp.astype(BF16), b_grp, scale, min(256, seq))
    y = y.reshape(m, d_inner)
    if out_norm_g is None:
        out = _matmul_res(y, w_out.astype(BF16), hm, min(512, m), 1024, "out_proj_b")
    else:
        out = _matmul_res_norm(y, w_out.astype(BF16), hm, out_norm_g, min(512, m),
                               "out_proj_b_norm")
    return out.reshape(b, seq, d_model)


def kernel(x, norm_a, w_in_a, kv_norm_a, kidx_norm_a, w_uk_a, w_uv_a, w_out_a, norm_b, w_in_b,
           w_grp_b, b_grp_b, scale_b, w_out_b, rel_bias, final_norm):
    n_a = norm_a.shape[0]
    n_b = norm_b.shape[0]
    depth = n_a + n_b
    h = x
    for layer in range(depth):
        j = layer // 2
        if layer % 2 == 0:
            h = _dsa_layer(h, norm_a[j], w_in_a[j], kv_norm_a[j], kidx_norm_a[j], w_uk_a[j],
                           w_uv_a[j], w_out_a[j], rel_bias)
        else:
            h = _pool_layer(h, norm_b[j], w_in_b[j], w_grp_b[j], b_grp_b[j], scale_b[j],
                            w_out_b[j], final_norm if layer == depth - 1 else None)
    if depth % 2 == 0:
        return h
    b, seq, d_model = h.shape
    out = _rmsnorm(h.reshape(b * seq, d_model), final_norm, min(1024, b * seq), "final_norm")
    return out.reshape(b, seq, d_model)
```

```python
import functools
import math

import numpy as np
import jax
import jax.numpy as jnp
from jax import lax
from jax.experimental import pallas as pl
from jax.experimental.pallas import tpu as pltpu

F32 = jnp.float32
BF16 = jnp.bfloat16

EPS = 1e-6
Q_BLOCK = 128
TOP_K_MAX = 256
REL_BUCKETS = 32
REL_MAX_DIST = 128
POOL_WINDOWS = (2, 4, 8, 16)
POOL_HALO = 16
NEG = -1e30
INT_MIN = -2 ** 31
LOG2E = math.log2(math.e)
INDEXER_ROWS = 256
INDEXER_SPAN = 256
NEAR_SUB = 2
VMEM_LIMIT = 56 * 1024 * 1024


def _cparams(n_grid):
    return pltpu.CompilerParams(dimension_semantics=("arbitrary",) * n_grid,
                                vmem_limit_bytes=VMEM_LIMIT)


def _silu(z):
    return z * (1.0 / (1.0 + jnp.exp(-z)))


def _norm_matmul_kernel(x_ref, g_ref, w_ref, o_ref, xn_ref):
    @pl.when(pl.program_id(1) == 0)
    def _():
        x = x_ref[...]
        ms = jnp.mean(x * x, axis=-1, keepdims=True)
        xn_ref[...] = (x * lax.rsqrt(ms + EPS) * g_ref[...]).astype(BF16)

    o_ref[...] = jnp.dot(xn_ref[...], w_ref[...],
                         preferred_element_type=F32).astype(o_ref.dtype)


def _norm_matmul(x, g, w, out_dtype, tm, tn, name):
    m, k = x.shape
    n = w.shape[1]
    tn = min(tn, n)
    return pl.pallas_call(
        _norm_matmul_kernel,
        grid=(m // tm, n // tn),
        in_specs=[pl.BlockSpec((tm, k), lambda i, j: (i, 0)),
                  pl.BlockSpec((1, k), lambda i, j: (0, 0)),
                  pl.BlockSpec((k, tn), lambda i, j: (0, j))],
        out_specs=pl.BlockSpec((tm, tn), lambda i, j: (i, j)),
        out_shape=jax.ShapeDtypeStruct((m, n), out_dtype),
        scratch_shapes=[pltpu.VMEM((tm, k), BF16)],
        compiler_params=_cparams(2),
        name=name,
    )(x, g.reshape(1, k), w)


def _matmul_res_kernel(y_ref, w_ref, r_ref, o_ref):
    o_ref[...] = jnp.dot(y_ref[...], w_ref[...], preferred_element_type=F32) + r_ref[...]


def _matmul_res(y, w, r, tm, tn, name):
    m, k = y.shape
    n = w.shape[1]
    tn = min(tn, n)
    return pl.pallas_call(
        _matmul_res_kernel,
        grid=(n // tn, m // tm),
        in_specs=[pl.BlockSpec((tm, k), lambda j, i: (i, 0)),
                  pl.BlockSpec((k, tn), lambda j, i: (0, j)),
                  pl.BlockSpec((tm, tn), lambda j, i: (i, j))],
        out_specs=pl.BlockSpec((tm, tn), lambda j, i: (i, j)),
        out_shape=jax.ShapeDtypeStruct((m, n), F32),
        compiler_params=_cparams(2),
        name=name,
    )(y, w, r)


def _matmul_res_norm_kernel(y_ref, w_ref, r_ref, g_ref, o_ref):
    h = jnp.dot(y_ref[...], w_ref[...], preferred_element_type=F32) + r_ref[...]
    ms = jnp.mean(h * h, axis=-1, keepdims=True)
    o_ref[...] = h * lax.rsqrt(ms + EPS) * g_ref[...]


def _matmul_res_norm(y, w, r, g, tm, name):
    m, k = y.shape
    n = w.shape[1]
    return pl.pallas_call(
        _matmul_res_norm_kernel,
        grid=(m // tm,),
        in_specs=[pl.BlockSpec((tm, k), lambda i: (i, 0)),
                  pl.BlockSpec((k, n), lambda i: (0, 0), pipeline_mode=pl.Buffered(1)),
                  pl.BlockSpec((tm, n), lambda i: (i, 0)),
                  pl.BlockSpec((1, n), lambda i: (0, 0))],
        out_specs=pl.BlockSpec((tm, n), lambda i: (i, 0)),
        out_shape=jax.ShapeDtypeStruct((m, n), F32),
        compiler_params=_cparams(1),
        name=name,
    )(y, w, r, g.reshape(1, n))


def _rmsnorm_kernel(x_ref, g_ref, o_ref):
    x = x_ref[...]
    ms = jnp.mean(x * x, axis=-1, keepdims=True)
    o_ref[...] = x * lax.rsqrt(ms + EPS) * g_ref[...]


def _rmsnorm(x, g, tm, name):
    m, k = x.shape
    return pl.pallas_call(
        _rmsnorm_kernel,
        grid=(m // tm,),
        in_specs=[pl.BlockSpec((tm, k), lambda i: (i, 0)),
                  pl.BlockSpec((1, k), lambda i: (0, 0))],
        out_specs=pl.BlockSpec((tm, k), lambda i: (i, 0)),
        out_shape=jax.ShapeDtypeStruct((m, k), F32),
        compiler_params=_cparams(1),
        name=name,
    )(x, g.reshape(1, k))


def _idx_proj_kernel(x_ref, g_ref, w_ref, kvn_ref, kin_ref, qi_ref, cn_ref, kn_ref, kw_ref):
    n_qi = qi_ref.shape[1]
    kv_lora = cn_ref.shape[1]
    idx_dim = kn_ref.shape[1]
    x = x_ref[...]
    ms = jnp.mean(x * x, axis=-1, keepdims=True)
    xn = (x * lax.rsqrt(ms + EPS) * g_ref[...]).astype(BF16)
    p = jnp.dot(xn, w_ref[...], preferred_element_type=F32)
    qi_ref[...] = p[:, :n_qi]
    c = p[:, n_qi:n_qi + kv_lora]
    kw = p[:, n_qi + kv_lora:]
    k = kw[:, :idx_dim]
    cn = c * lax.rsqrt(jnp.mean(c * c, axis=-1, keepdims=True) + EPS) * kvn_ref[...]
    kn = k * lax.rsqrt(jnp.mean(k * k, axis=-1, keepdims=True) + EPS) * kin_ref[...]
    cn_ref[...] = cn.astype(BF16)
    kn_ref[...] = kn.astype(BF16)
    kw_ref[...] = kw


def _idx_proj(x, g, w, kv_norm, kidx_norm, n_qi, tm):
    m, k = x.shape
    n = w.shape[1]
    kv_lora = kv_norm.shape[0]
    idx_dim = kidx_norm.shape[0]
    n_kw = n - n_qi - kv_lora
    return pl.pallas_call(
        _idx_proj_kernel,
        grid=(m // tm,),
        in_specs=[pl.BlockSpec((tm, k), lambda i: (i, 0)),
                  pl.BlockSpec((1, k), lambda i: (0, 0)),
                  pl.BlockSpec((k, n), lambda i: (0, 0)),
                  pl.BlockSpec((1, kv_lora), lambda i: (0, 0)),
                  pl.BlockSpec((1, idx_dim), lambda i: (0, 0))],
        out_specs=[pl.BlockSpec((tm, n_qi), lambda i: (i, 0)),
                   pl.BlockSpec((tm, kv_lora), lambda i: (i, 0)),
                   pl.BlockSpec((tm, idx_dim), lambda i: (i, 0)),
                   pl.BlockSpec((tm, n_kw), lambda i: (i, 0))],
        out_shape=[jax.ShapeDtypeStruct((m, n_qi), F32),
                   jax.ShapeDtypeStruct((m, kv_lora), BF16),
                   jax.ShapeDtypeStruct((m, idx_dim), BF16),
                   jax.ShapeDtypeStruct((m, n_kw), F32)],
        compiler_params=_cparams(1),
        name="in_proj_a_idx",
    )(x, g.reshape(1, k), w, kv_norm.reshape(1, kv_lora), kidx_norm.reshape(1, idx_dim))


def _indexer_kernel(qi_ref, sm_ref, kn_ref, mask_ref, *, span, **kw):
    i = pl.program_id(1)
    rows = qi_ref.shape[1]
    total = kn_ref.shape[1]
    for v in range(1, total // span + 1):
        @pl.when(((i + 1) * rows + span - 1) // span == v)
        def _(v=v):
            _indexer_extent(qi_ref, sm_ref, kn_ref, mask_ref, v * span, **kw)


def _indexer_extent(qi_ref, sm_ref, kn_ref, mask_ref, seq, *, top_k, n_heads, idx_dim, w_off):
    i = pl.program_id(1)
    rows = qi_ref.shape[1]
    n_sub = seq // Q_BLOCK
    q = qi_ref[0]
    w = sm_ref[0][:, w_off:w_off + n_heads] * (n_heads ** -0.5 * idx_dim ** -0.5)
    kn = kn_ref[0, :seq, :]

    score = jnp.zeros((rows, seq), F32)
    for h in range(n_heads):
        qh = q[:, h * idx_dim:(h + 1) * idx_dim].astype(BF16)
        s = lax.dot_general(qh, kn, (((1,), (1,)), ((), ())), preferred_element_type=F32)
        score = score + w[:, h:h + 1] * jnp.maximum(s, 0.0)

    bits = pltpu.bitcast(score + 0.0, jnp.int32)
    key = jnp.where(bits >= 0, bits, bits ^ jnp.int32(0x7FFFFFFF))
    t_pos = i * rows + lax.broadcasted_iota(jnp.int32, (rows, seq), 0)
    s_pos = lax.broadcasted_iota(jnp.int32, (rows, seq), 1)
    causal = s_pos <= t_pos
    key = jnp.where(causal, key, jnp.int32(INT_MIN))

    key_t = key.T

    def count_ge(thr_row):
        return jnp.sum(jnp.where(key_t >= thr_row, 1.0, 0.0), axis=0, keepdims=True)

    kf = float(top_k)
    thr_row = jnp.where(count_ge(jnp.zeros((1, rows), jnp.int32)) >= kf,
                        jnp.int32(0), jnp.int32(INT_MIN))

    def bit_step(b, thr_row):
        cand = thr_row + jnp.left_shift(jnp.int32(1), 30 - b)
        return jnp.where(count_ge(cand) >= kf, cand, thr_row)

    thr_row = lax.fori_loop(0, 31, bit_step, thr_row)
    thr = jnp.broadcast_to(thr_row, (Q_BLOCK, rows)).T[:, :1]

    ge = key >= thr
    sel = jnp.where(causal, jnp.where(ge, 1.0, 0.0), 0.0)
    def write_mask(j, cols):
        for r in range(rows // Q_BLOCK):
            mask_ref[0, r, j] = cols[r * Q_BLOCK:(r + 1) * Q_BLOCK]

    for j in range(n_sub):
        blk = sel[:, j * Q_BLOCK:(j + 1) * Q_BLOCK]
        write_mask(j, jnp.where(blk > 0.5, 0.0, NEG).astype(BF16))
    for j in range(n_sub, mask_ref.shape[2]):
        write_mask(j, jnp.full((rows, Q_BLOCK), NEG, BF16))

    n_ge = jnp.sum(jnp.where(ge, 1.0, 0.0), axis=-1, keepdims=True)
    has_tie = jnp.logical_and(n_ge > kf, thr > jnp.int32(INT_MIN))
    any_tie = jnp.max(jnp.where(has_tie, 1.0, 0.0))

    @pl.when(any_tie > 0.0)
    def _():
        cw = 2 * Q_BLOCK
        gt = key > thr
        eq = key == thr
        need = kf - jnp.sum(jnp.where(gt, 1.0, 0.0), axis=-1, keepdims=True)
        r_id = lax.broadcasted_iota(jnp.int32, (cw, cw), 0)
        c_id = lax.broadcasted_iota(jnp.int32, (cw, cw), 1)
        upper = jnp.where(r_id < c_id, 1.0, 0.0).astype(BF16)
        carry = jnp.zeros((rows, 1), F32)
        for c in range(seq // cw):
            sl = slice(c * cw, (c + 1) * cw)
            eqc = jnp.where(eq[:, sl], 1.0, 0.0)
            rank = carry + jnp.dot(eqc.astype(BF16), upper, preferred_element_type=F32)
            keep = jnp.where(gt[:, sl], 1.0, jnp.where(rank < need, eqc, 0.0))
            keep = jnp.where(has_tie, keep, sel[:, sl])
            carry = carry + jnp.sum(eqc, axis=-1, keepdims=True)
            add = jnp.where(keep > 0.5, 0.0, NEG).astype(BF16)
            write_mask(2 * c, add[:, :Q_BLOCK])
            write_mask(2 * c + 1, add[:, Q_BLOCK:])


def _indexer(qi, small, kn, top_k, n_heads, idx_dim, w_off):
    b, seq, _ = qi.shape
    n_blk = seq // Q_BLOCK
    rows = min(INDEXER_ROWS, seq)
    return pl.pallas_call(
        functools.partial(_indexer_kernel, span=min(INDEXER_SPAN, seq), top_k=top_k,
                          n_heads=n_heads, idx_dim=idx_dim, w_off=w_off),
        grid=(b, seq // rows),
        in_specs=[pl.BlockSpec((1, rows, qi.shape[2]), lambda bb, i: (bb, i, 0)),
                  pl.BlockSpec((1, rows, small.shape[2]), lambda bb, i: (bb, i, 0)),
                  pl.BlockSpec((1, seq, idx_dim), lambda bb, i: (bb, 0, 0))],
        out_specs=pl.BlockSpec((1, rows // Q_BLOCK, n_blk, Q_BLOCK, Q_BLOCK),
                               lambda bb, i: (bb, i, 0, 0, 0)),
        out_shape=jax.ShapeDtypeStruct((b, n_blk, n_blk, Q_BLOCK, Q_BLOCK), BF16),
        compiler_params=_cparams(2),
        name="indexer_mask",
    )(qi, small, kn)


def _attn_kernel(q_ref, z_ref, cn_ref, mask_ref, wuk_ref, wuv_ref, bias_ref, y_ref,
                 qlat_ref, s_ref, acc_ref, m_ref, l_ref, *, n_grp, head_dim, scale):
    i = pl.program_id(2)
    rows = n_grp * Q_BLOCK

    for h in range(n_grp):
        qh = q_ref[0, :, h * head_dim:(h + 1) * head_dim]
        ql = jnp.dot(qh, wuk_ref[h], preferred_element_type=F32) * scale
        qlat_ref[h * Q_BLOCK:(h + 1) * Q_BLOCK, :] = ql.astype(BF16)
    m_ref[...] = jnp.full(m_ref.shape, NEG, F32)

    def score_chunk(sub0, n_sb, n_bias=0):
        width = n_sb * Q_BLOCK
        kstart = pl.multiple_of(sub0 * Q_BLOCK, Q_BLOCK)
        kc = cn_ref[0, pl.ds(kstart, width), :]
        s = lax.dot_general(qlat_ref[...], kc, (((1,), (1,)), ((), ())),
                            preferred_element_type=F32)
        s3 = s.reshape(n_grp, Q_BLOCK, width)
        mx = m_ref[...]
        for j in range(n_sb):
            blk = s3[:, :, j * Q_BLOCK:(j + 1) * Q_BLOCK] + mask_ref[0, 0, sub0 + j].astype(F32)[None]
            jb = j - (n_sb - n_bias)
            if jb >= 0:
                lo = (NEAR_SUB - n_bias + jb) * Q_BLOCK
                blk = blk + bias_ref[:, :, lo:lo + Q_BLOCK]
            blk = blk.reshape(rows, Q_BLOCK)
            s_ref[sub0 + j] = blk
            mx = jnp.maximum(mx, blk)
        m_ref[...] = mx

    def quads_then_tail(n_far, chunk, finish):
        n_quad = n_far // 4

        def quad(j, carry):
            chunk(4 * j, 4, 0)
            return carry

        lax.fori_loop(0, n_quad, quad, 0)

        @pl.when(i == 0)
        def _():
            chunk(0, 1, 1)
            finish()

        for rem in range(4):
            @pl.when(jnp.logical_and(i > 0, n_far - 4 * n_quad == rem))
            def _(rem=rem):
                chunk(4 * n_quad, rem + NEAR_SUB, NEAR_SUB)
                finish()

    def finish_scores():
        m_ref[...] = jnp.broadcast_to(jnp.max(m_ref[...], axis=-1, keepdims=True), m_ref.shape)
        l_ref[...] = jnp.zeros(l_ref.shape, F32)
        acc_ref[...] = jnp.zeros(acc_ref.shape, F32)

    def pv_chunk(sub0, n_sb, n_bias=0):
        kstart = pl.multiple_of(sub0 * Q_BLOCK, Q_BLOCK)
        kc = cn_ref[0, pl.ds(kstart, n_sb * Q_BLOCK), :]
        m = m_ref[...]
        ps = [jnp.exp2(s_ref[sub0 + j] - m) for j in range(n_sb)]
        l_ref[...] += functools.reduce(lambda a, b: a + b, ps)
        p = jnp.concatenate([pj.astype(BF16) for pj in ps], axis=-1)
        acc_ref[...] += jnp.dot(p, kc, preferred_element_type=F32)

    def finish_output():
        l = jnp.sum(l_ref[...], axis=-1, keepdims=True)
        for h in range(n_grp):
            hs = slice(h * Q_BLOCK, (h + 1) * Q_BLOCK)
            oh = jnp.dot(acc_ref[hs, :].astype(BF16), wuv_ref[h],
                         preferred_element_type=F32) / l[hs]
            zz = z_ref[0, :, h * head_dim:(h + 1) * head_dim].astype(F32)
            y_ref[0, :, h * head_dim:(h + 1) * head_dim] = (oh * _silu(zz)).astype(BF16)

    n_far = jnp.maximum(i - 1, 0)
    quads_then_tail(n_far, score_chunk, finish_scores)
    quads_then_tail(n_far, pv_chunk, finish_output)


def _attention(qz, cn, mask, wuk_t, wuv_h, bias_tile, n_grp):
    b, seq, two_inner = qz.shape
    d_inner = two_inner // 2
    n_heads, head_dim, kv_lora = wuk_t.shape
    n_blk = seq // Q_BLOCK
    n_g = n_heads // n_grp
    gw = n_grp * head_dim
    rows = n_grp * Q_BLOCK
    return pl.pallas_call(
        functools.partial(_attn_kernel, n_grp=n_grp, head_dim=head_dim,
                          scale=head_dim ** -0.5 * LOG2E),
        grid=(n_g, b, n_blk),
        in_specs=[pl.BlockSpec((1, Q_BLOCK, gw), lambda g, bb, i: (bb, i, g)),
                  pl.BlockSpec((1, Q_BLOCK, gw), lambda g, bb, i: (bb, i, n_g + g)),
                  pl.BlockSpec((1, seq, kv_lora), lambda g, bb, i: (bb, 0, 0)),
                  pl.BlockSpec((1, 1, n_blk, Q_BLOCK, Q_BLOCK), lambda g, bb, i: (bb, i, 0, 0, 0)),
                  pl.BlockSpec((n_grp, head_dim, kv_lora), lambda g, bb, i: (g, 0, 0)),
                  pl.BlockSpec((n_grp, kv_lora, head_dim), lambda g, bb, i: (g, 0, 0)),
                  pl.BlockSpec((n_grp, Q_BLOCK, NEAR_SUB * Q_BLOCK), lambda g, bb, i: (g, 0, 0))],
        out_specs=pl.BlockSpec((1, Q_BLOCK, gw), lambda g, bb, i: (bb, i, g)),
        out_shape=jax.ShapeDtypeStruct((b, seq, d_inner), BF16),
        scratch_shapes=[pltpu.VMEM((rows, kv_lora), BF16),
                        pltpu.VMEM((n_blk, rows, Q_BLOCK), F32),
                        pltpu.VMEM((rows, kv_lora), F32),
                        pltpu.VMEM((rows, Q_BLOCK), F32),
                        pltpu.VMEM((rows, Q_BLOCK), F32)],
        compiler_params=_cparams(3),
        name="latent_attention",
    )(qz, qz, cn, mask, wuk_t, wuv_h, bias_tile * LOG2E)


def _t5_bucket_table(n):
    max_exact = REL_BUCKETS // 2
    d = np.arange(n)
    df = np.maximum(d, 1).astype(np.float64)
    large = max_exact + (np.log(df / max_exact) / math.log(REL_MAX_DIST / max_exact)
                         * (REL_BUCKETS - max_exact)).astype(np.int64)
    large = np.minimum(large, REL_BUCKETS - 1)
    return np.where(d < max_exact, d, large).astype(np.int32)


def _bias_tile(rel_bias):
    width = NEAR_SUB * Q_BLOCK
    period = width + Q_BLOCK
    table = _t5_bucket_table(period)
    assert (table[Q_BLOCK:] == REL_BUCKETS - 1).all()
    k = np.arange(period)
    dist = np.where(k <= width, (NEAR_SUB - 1) * Q_BLOCK - k, (NEAR_SUB - 1) * Q_BLOCK + period - k)
    w = rel_bias[table[np.maximum(dist, 0)]] - rel_bias[REL_BUCKETS - 1]
    flat = jnp.tile(w.T, (1, Q_BLOCK))[:, :Q_BLOCK * (period - 1)]
    return flat.reshape(-1, Q_BLOCK, period - 1)[:, :, :width].astype(F32)


def _pool_kernel(u_ref, z_ref, w_ref, b_ref, sc_ref, y_ref, *, tr):
    g = pl.program_id(1)
    for gi, window in enumerate(POOL_WINDOWS):
        @pl.when(g == gi)
        def _(window=window):
            _pool_group(u_ref, z_ref, w_ref, b_ref, sc_ref, y_ref, window, tr)


def _pool_group(u_ref, z_ref, w_ref, b_ref, sc_ref, y_ref, window, tr):
    seq = u_ref.shape[1]
    pg = u_ref.shape[2]

    def window_sum(ext):
        acc, span = ext, 1
        while span < window:
            acc = acc + pltpu.roll(acc, span, 0)
            span *= 2
        return acc[POOL_HALO:]

    def mix_and_gate(pooled, r0):
        mixed = jnp.dot(pooled.astype(BF16), w_ref[0], preferred_element_type=F32) + b_ref[0]
        mixed = mixed * sc_ref[0]
        zz = z_ref[0, pl.ds(r0, tr), :].astype(F32)
        y_ref[0, pl.ds(r0, tr), :] = (mixed * _silu(zz)).astype(BF16)

    first = jnp.concatenate([jnp.zeros((POOL_HALO, pg), F32), u_ref[0, 0:tr, :].astype(F32)],
                            axis=0)
    t = lax.broadcasted_iota(jnp.int32, (tr, 1), 0)
    cnt = jnp.minimum(t + 1, window).astype(F32)
    mix_and_gate(window_sum(first) / cnt - first[POOL_HALO:], 0)

    def body(r, carry):
        r0 = pl.multiple_of(r * tr, tr)
        ext = u_ref[0, pl.ds(pl.multiple_of(r0 - POOL_HALO, POOL_HALO), tr + POOL_HALO), :]
        ext = ext.astype(F32)
        mix_and_gate(window_sum(ext) * (1.0 / window) - ext[POOL_HALO:], r0)
        return carry

    lax.fori_loop(1, seq // tr, body, 0)


def _pool_mix(u, z, col0, w_grp, b_grp, scale, tr):
    b, seq, _ = u.shape
    n_groups, pg, _ = w_grp.shape
    d_inner = n_groups * pg
    zoff = col0 // pg
    return pl.pallas_call(
        functools.partial(_pool_kernel, tr=tr),
        grid=(b, n_groups),
        in_specs=[pl.BlockSpec((1, seq, pg), lambda bb, g: (bb, 0, g)),
                  pl.BlockSpec((1, seq, pg), lambda bb, g: (bb, 0, zoff + g)),
                  pl.BlockSpec((1, pg, pg), lambda bb, g: (g, 0, 0)),
                  pl.BlockSpec((1, 1, pg), lambda bb, g: (g, 0, 0)),
                  pl.BlockSpec((1, 1, pg), lambda bb, g: (g, 0, 0))],
        out_specs=pl.BlockSpec((1, seq, pg), lambda bb, g: (bb, 0, g)),
        out_shape=jax.ShapeDtypeStruct((b, seq, d_inner), BF16),
        compiler_params=_cparams(2),
        name="pool_mix",
    )(u, z, w_grp, b_grp.reshape(n_groups, 1, pg), scale.reshape(n_groups, 1, pg))


def _dsa_layer(h, norm_g, w_in, kv_norm, kidx_norm, w_uk, w_uv, w_out, rel_bias):
    b, seq, d_model = h.shape
    d_inner = w_out.shape[0]
    kv_lora, n_heads, head_dim = w_uk.shape
    idx_dim = kidx_norm.shape[0]
    a_in = w_in.shape[1]
    idx_heads = (a_in - 2 * d_inner - kv_lora - idx_dim) // (idx_dim + 1)
    o_ckv = d_inner
    o_qi = o_ckv + kv_lora
    o_ki = o_qi + idx_heads * idx_dim
    o_wi = o_ki + idx_dim
    o_z = o_wi + idx_heads
    assert o_z + d_inner == a_in and head_dim == Q_BLOCK and n_heads * head_dim == d_inner
    top_k = min(TOP_K_MAX, seq // 4)
    m = b * seq
    hm = h.reshape(m, d_model)

    w_qz = jnp.concatenate([w_in[:, :d_inner].astype(BF16), w_in[:, o_z:].astype(BF16)], axis=1)
    n_qi = idx_heads * idx_dim
    pad = (-(idx_dim + idx_heads)) % Q_BLOCK
    w_idx = jnp.concatenate([w_in[:, o_qi:o_ki].astype(BF16), w_in[:, o_ckv:o_qi].astype(BF16),
                             w_in[:, o_ki:o_z].astype(BF16), jnp.zeros((d_model, pad), BF16)],
                            axis=1)

    qz = _norm_matmul(hm, norm_g, w_qz, BF16, min(1024, m), 2048, "in_proj_a_qz")
    qi, cn, kn, kw = _idx_proj(hm, norm_g, w_idx, kv_norm, kidx_norm, n_qi, min(512, m))
    mask = _indexer(qi.reshape(b, seq, -1), kw.reshape(b, seq, -1), kn.reshape(b, seq, idx_dim),
                    top_k, idx_heads, idx_dim, idx_dim)

    wuk_t = jnp.transpose(w_uk, (1, 2, 0)).astype(BF16)
    wuv_h = jnp.transpose(w_uv, (1, 0, 2)).astype(BF16)
    n_grp = min(16, n_heads)
    y = _attention(qz.reshape(b, seq, 2 * d_inner), cn.reshape(b, seq, kv_lora), mask,
                   wuk_t, wuv_h, _bias_tile(rel_bias), n_grp)
    out = _matmul_res(y.reshape(m, d_inner), w_out.astype(BF16), hm, min(512, m), 1024,
                      "out_proj_a")
    return out.reshape(b, seq, d_model)


def _pool_layer(h, norm_g, w_in, w_grp, b_grp, scale, w_out, out_norm_g):
    b, seq, d_model = h.shape
    d_inner = w_out.shape[0]
    assert w_grp.shape[0] == len(POOL_WINDOWS) and POOL_WINDOWS == (2, 4, 8, 16)
    m = b * seq
    hm = h.reshape(m, d_model)
    uz = _norm_matmul(hm, norm_g, w_in.astype(BF16), BF16, min(1024, m), 2048, "in_proj_b")
    uz = uz.reshape(b, seq, 2 * d_inner)
    y = _pool_mix(uz, uz, d_inner, w_grp.astype(BF16), b_grp, scale, min(256, seq))
    y = y.reshape(m, d_inner)
    if out_norm_g is None:
        out = _matmul_res(y, w_out.astype(BF16), hm, min(512, m), 1024, "out_proj_b")
    else:
        out = _matmul_res_norm(y, w_out.astype(BF16), hm, out_norm_g, min(512, m),
                               "out_proj_b_norm")
    return out.reshape(b, seq, d_model)


def kernel(x, norm_a, w_in_a, kv_norm_a, kidx_norm_a, w_uk_a, w_uv_a, w_out_a, norm_b, w_in_b,
           w_grp_b, b_grp_b, scale_b, w_out_b, rel_bias, final_norm):
    n_a = norm_a.shape[0]
    n_b = norm_b.shape[0]
    depth = n_a + n_b
    h = x
    for layer in range(depth):
        j = layer // 2
        if layer % 2 == 0:
            h = _dsa_layer(h, norm_a[j], w_in_a[j], kv_norm_a[j], kidx_norm_a[j], w_uk_a[j],
                           w_uv_a[j], w_out_a[j], rel_bias)
        else:
            h = _pool_layer(h, norm_b[j], w_in_b[j], w_grp_b[j], b_grp_b[j], scale_b[j],
                            w_out_b[j], final_norm if layer == depth - 1 else None)
    if depth % 2 == 0:
        return h
    b, seq, d_model = h.shape
    out = _rmsnorm(h.reshape(b * seq, d_model), final_norm, min(1024, b * seq), "final_norm")
    return out.reshape(b, seq, d_model)
```

```python
import functools
import math

import numpy as np
import jax
import jax.numpy as jnp
from jax import lax
from jax.experimental import pallas as pl
from jax.experimental.pallas import tpu as pltpu

F32 = jnp.float32
BF16 = jnp.bfloat16

EPS = 1e-6
Q_BLOCK = 128
TOP_K_MAX = 256
REL_BUCKETS = 32
REL_MAX_DIST = 128
POOL_WINDOWS = (2, 4, 8, 16)
POOL_HALO = 16
NEG = -1e30
INT_MIN = -2 ** 31
LOG2E = math.log2(math.e)
NEAR_SUB = 2

VMEM_LIMIT = 56 * 1024 * 1024
PROJ_ROWS = 1024
PROJ_COLS = 2048
OUT_COLS = 1024
NORM_OUT_ROWS = 512
POOL_ROWS = 1024
INDEXER_ROWS = 256
INDEXER_SPAN = 256
ATTN_HEADS = 16


def _cparams(n_grid):
    return pltpu.CompilerParams(dimension_semantics=("arbitrary",) * n_grid,
                                vmem_limit_bytes=VMEM_LIMIT)


def _silu(z):
    return z * (1.0 / (1.0 + jnp.exp(-z)))


def _norm_matmul_kernel(x_ref, g_ref, w_ref, o_ref, xn_ref):
    @pl.when(pl.program_id(1) == 0)
    def _():
        x = x_ref[...]
        ms = jnp.mean(x * x, axis=-1, keepdims=True)
        xn_ref[...] = (x * lax.rsqrt(ms + EPS) * g_ref[...]).astype(BF16)

    o_ref[...] = jnp.dot(xn_ref[...], w_ref[...],
                         preferred_element_type=F32).astype(o_ref.dtype)


def _norm_matmul(x, g, w, out_dtype, tm, tn, name):
    m, k = x.shape
    n = w.shape[1]
    tn = min(tn, n)
    return pl.pallas_call(
        _norm_matmul_kernel,
        grid=(m // tm, n // tn),
        in_specs=[pl.BlockSpec((tm, k), lambda i, j: (i, 0)),
                  pl.BlockSpec((1, k), lambda i, j: (0, 0)),
                  pl.BlockSpec((k, tn), lambda i, j: (0, j))],
        out_specs=pl.BlockSpec((tm, tn), lambda i, j: (i, j)),
        out_shape=jax.ShapeDtypeStruct((m, n), out_dtype),
        scratch_shapes=[pltpu.VMEM((tm, k), BF16)],
        compiler_params=_cparams(2),
        name=name,
    )(x, g.reshape(1, k), w)


def _matmul_res_kernel(y_ref, w_ref, r_ref, o_ref):
    o_ref[...] = jnp.dot(y_ref[...], w_ref[...], preferred_element_type=F32) + r_ref[...]


def _matmul_res(y, w, r, tm, tn, name):
    m, k = y.shape
    n = w.shape[1]
    tn = min(tn, n)
    return pl.pallas_call(
        _matmul_res_kernel,
        grid=(n // tn, m // tm),
        in_specs=[pl.BlockSpec((tm, k), lambda j, i: (i, 0)),
                  pl.BlockSpec((k, tn), lambda j, i: (0, j)),
                  pl.BlockSpec((tm, tn), lambda j, i: (i, j))],
        out_specs=pl.BlockSpec((tm, tn), lambda j, i: (i, j)),
        out_shape=jax.ShapeDtypeStruct((m, n), F32),
        compiler_params=_cparams(2),
        name=name,
    )(y, w, r)


def _matmul_res_norm_kernel(y_ref, w_ref, r_ref, g_ref, o_ref):
    h = jnp.dot(y_ref[...], w_ref[...], preferred_element_type=F32) + r_ref[...]
    ms = jnp.mean(h * h, axis=-1, keepdims=True)
    o_ref[...] = h * lax.rsqrt(ms + EPS) * g_ref[...]


def _matmul_res_norm(y, w, r, g, tm, name):
    m, k = y.shape
    n = w.shape[1]
    return pl.pallas_call(
        _matmul_res_norm_kernel,
        grid=(m // tm,),
        in_specs=[pl.BlockSpec((tm, k), lambda i: (i, 0)),
                  pl.BlockSpec((k, n), lambda i: (0, 0), pipeline_mode=pl.Buffered(1)),
                  pl.BlockSpec((tm, n), lambda i: (i, 0)),
                  pl.BlockSpec((1, n), lambda i: (0, 0))],
        out_specs=pl.BlockSpec((tm, n), lambda i: (i, 0)),
        out_shape=jax.ShapeDtypeStruct((m, n), F32),
        compiler_params=_cparams(1),
        name=name,
    )(y, w, r, g.reshape(1, n))


def _rmsnorm_kernel(x_ref, g_ref, o_ref):
    x = x_ref[...]
    ms = jnp.mean(x * x, axis=-1, keepdims=True)
    o_ref[...] = x * lax.rsqrt(ms + EPS) * g_ref[...]


def _rmsnorm(x, g, tm, name):
    m, k = x.shape
    return pl.pallas_call(
        _rmsnorm_kernel,
        grid=(m // tm,),
        in_specs=[pl.BlockSpec((tm, k), lambda i: (i, 0)),
                  pl.BlockSpec((1, k), lambda i: (0, 0))],
        out_specs=pl.BlockSpec((tm, k), lambda i: (i, 0)),
        out_shape=jax.ShapeDtypeStruct((m, k), F32),
        compiler_params=_cparams(1),
        name=name,
    )(x, g.reshape(1, k))


def _idx_proj_kernel(x_ref, g_ref, w_ref, kvn_ref, kin_ref, qi_ref, cn_ref, kn_ref, kw_ref):
    n_qi = qi_ref.shape[1]
    kv_lora = cn_ref.shape[1]
    idx_dim = kn_ref.shape[1]
    x = x_ref[...]
    ms = jnp.mean(x * x, axis=-1, keepdims=True)
    xn = (x * lax.rsqrt(ms + EPS) * g_ref[...]).astype(BF16)
    p = jnp.dot(xn, w_ref[...], preferred_element_type=F32)
    qi_ref[...] = p[:, :n_qi]
    c = p[:, n_qi:n_qi + kv_lora]
    kw = p[:, n_qi + kv_lora:]
    k = kw[:, :idx_dim]
    cn = c * lax.rsqrt(jnp.mean(c * c, axis=-1, keepdims=True) + EPS) * kvn_ref[...]
    kn = k * lax.rsqrt(jnp.mean(k * k, axis=-1, keepdims=True) + EPS) * kin_ref[...]
    cn_ref[...] = cn.astype(BF16)
    kn_ref[...] = kn.astype(BF16)
    kw_ref[...] = kw


def _idx_proj(x, g, w, kv_norm, kidx_norm, n_qi, tm):
    m, k = x.shape
    n = w.shape[1]
    kv_lora = kv_norm.shape[0]
    idx_dim = kidx_norm.shape[0]
    n_kw = n - n_qi - kv_lora
    return pl.pallas_call(
        _idx_proj_kernel,
        grid=(m // tm,),
        in_specs=[pl.BlockSpec((tm, k), lambda i: (i, 0)),
                  pl.BlockSpec((1, k), lambda i: (0, 0)),
                  pl.BlockSpec((k, n), lambda i: (0, 0)),
                  pl.BlockSpec((1, kv_lora), lambda i: (0, 0)),
                  pl.BlockSpec((1, idx_dim), lambda i: (0, 0))],
        out_specs=[pl.BlockSpec((tm, n_qi), lambda i: (i, 0)),
                   pl.BlockSpec((tm, kv_lora), lambda i: (i, 0)),
                   pl.BlockSpec((tm, idx_dim), lambda i: (i, 0)),
                   pl.BlockSpec((tm, n_kw), lambda i: (i, 0))],
        out_shape=[jax.ShapeDtypeStruct((m, n_qi), F32),
                   jax.ShapeDtypeStruct((m, kv_lora), BF16),
                   jax.ShapeDtypeStruct((m, idx_dim), BF16),
                   jax.ShapeDtypeStruct((m, n_kw), F32)],
        compiler_params=_cparams(1),
        name="in_proj_a_idx",
    )(x, g.reshape(1, k), w, kv_norm.reshape(1, kv_lora), kidx_norm.reshape(1, idx_dim))


def _indexer_kernel(qi_ref, sm_ref, kn_ref, mask_ref, *, span, **kw):
    i = pl.program_id(1)
    rows = qi_ref.shape[1]
    total = kn_ref.shape[1]
    for v in range(1, total // span + 1):
        @pl.when(((i + 1) * rows + span - 1) // span == v)
        def _(v=v):
            _indexer_extent(qi_ref, sm_ref, kn_ref, mask_ref, v * span, **kw)


def _indexer_extent(qi_ref, sm_ref, kn_ref, mask_ref, seq, *, top_k, n_heads, idx_dim, w_off):
    i = pl.program_id(1)
    rows = qi_ref.shape[1]
    n_sub = seq // Q_BLOCK
    q = qi_ref[0]
    w = sm_ref[0][:, w_off:w_off + n_heads] * (n_heads ** -0.5 * idx_dim ** -0.5)
    kn = kn_ref[0, :seq, :]

    score = jnp.zeros((rows, seq), F32)
    for h in range(n_heads):
        qh = q[:, h * idx_dim:(h + 1) * idx_dim].astype(BF16)
        s = lax.dot_general(qh, kn, (((1,), (1,)), ((), ())), preferred_element_type=F32)
        score = score + w[:, h:h + 1] * jnp.maximum(s, 0.0)

    bits = pltpu.bitcast(score + 0.0, jnp.int32)
    key = jnp.where(bits >= 0, bits, bits ^ jnp.int32(0x7FFFFFFF))
    t_pos = i * rows + lax.broadcasted_iota(jnp.int32, (rows, seq), 0)
    s_pos = lax.broadcasted_iota(jnp.int32, (rows, seq), 1)
    causal = s_pos <= t_pos
    key = jnp.where(causal, key, jnp.int32(INT_MIN))

    key_t = key.T

    def count_ge(thr_row):
        return jnp.sum(jnp.where(key_t >= thr_row, 1.0, 0.0), axis=0, keepdims=True)

    kf = float(top_k)
    thr_row = jnp.where(count_ge(jnp.zeros((1, rows), jnp.int32)) >= kf,
                        jnp.int32(0), jnp.int32(INT_MIN))

    def bit_step(b, thr_row):
        cand = thr_row + jnp.left_shift(jnp.int32(1), 30 - b)
        return jnp.where(count_ge(cand) >= kf, cand, thr_row)

    thr_row = lax.fori_loop(0, 31, bit_step, thr_row)
    thr = jnp.broadcast_to(thr_row, (Q_BLOCK, rows)).T[:, :1]

    ge = key >= thr
    sel = jnp.where(causal, jnp.where(ge, 1.0, 0.0), 0.0)
    def write_mask(j, cols):
        for r in range(rows // Q_BLOCK):
            mask_ref[0, r, j] = cols[r * Q_BLOCK:(r + 1) * Q_BLOCK]

    for j in range(n_sub):
        blk = sel[:, j * Q_BLOCK:(j + 1) * Q_BLOCK]
        write_mask(j, jnp.where(blk > 0.5, 0.0, NEG).astype(BF16))
    for j in range(n_sub, mask_ref.shape[2]):
        write_mask(j, jnp.full((rows, Q_BLOCK), NEG, BF16))

    n_ge = jnp.sum(jnp.where(ge, 1.0, 0.0), axis=-1, keepdims=True)
    has_tie = jnp.logical_and(n_ge > kf, thr > jnp.int32(INT_MIN))
    any_tie = jnp.max(jnp.where(has_tie, 1.0, 0.0))

    @pl.when(any_tie > 0.0)
    def _():
        cw = 2 * Q_BLOCK
        gt = key > thr
        eq = key == thr
        need = kf - jnp.sum(jnp.where(gt, 1.0, 0.0), axis=-1, keepdims=True)
        r_id = lax.broadcasted_iota(jnp.int32, (cw, cw), 0)
        c_id = lax.broadcasted_iota(jnp.int32, (cw, cw), 1)
        upper = jnp.where(r_id < c_id, 1.0, 0.0).astype(BF16)
        carry = jnp.zeros((rows, 1), F32)
        for c in range(seq // cw):
            sl = slice(c * cw, (c + 1) * cw)
            eqc = jnp.where(eq[:, sl], 1.0, 0.0)
            rank = carry + jnp.dot(eqc.astype(BF16), upper, preferred_element_type=F32)
            keep = jnp.where(gt[:, sl], 1.0, jnp.where(rank < need, eqc, 0.0))
            keep = jnp.where(has_tie, keep, sel[:, sl])
            carry = carry + jnp.sum(eqc, axis=-1, keepdims=True)
            add = jnp.where(keep > 0.5, 0.0, NEG).astype(BF16)
            write_mask(2 * c, add[:, :Q_BLOCK])
            write_mask(2 * c + 1, add[:, Q_BLOCK:])


def _indexer(qi, small, kn, top_k, n_heads, idx_dim, w_off):
    b, seq, _ = qi.shape
    n_blk = seq // Q_BLOCK
    rows = min(INDEXER_ROWS, seq)
    return pl.pallas_call(
        functools.partial(_indexer_kernel, span=min(INDEXER_SPAN, seq), top_k=top_k,
                          n_heads=n_heads, idx_dim=idx_dim, w_off=w_off),
        grid=(b, seq // rows),
        in_specs=[pl.BlockSpec((1, rows, qi.shape[2]), lambda bb, i: (bb, i, 0)),
                  pl.BlockSpec((1, rows, small.shape[2]), lambda bb, i: (bb, i, 0)),
                  pl.BlockSpec((1, seq, idx_dim), lambda bb, i: (bb, 0, 0))],
        out_specs=pl.BlockSpec((1, rows // Q_BLOCK, n_blk, Q_BLOCK, Q_BLOCK),
                               lambda bb, i: (bb, i, 0, 0, 0)),
        out_shape=jax.ShapeDtypeStruct((b, n_blk, n_blk, Q_BLOCK, Q_BLOCK), BF16),
        compiler_params=_cparams(2),
        name="indexer_mask",
    )(qi, small, kn)


def _attn_kernel(q_ref, z_ref, cn_ref, mask_ref, wuk_ref, wuv_ref, bias_ref, y_ref,
                 qlat_ref, s_ref, acc_ref, m_ref, l_ref, *, n_grp, head_dim, scale):
    i = pl.program_id(2)
    rows = n_grp * Q_BLOCK

    for h in range(n_grp):
        qh = q_ref[0, :, h * head_dim:(h + 1) * head_dim]
        ql = jnp.dot(qh, wuk_ref[h], preferred_element_type=F32) * scale
        qlat_ref[h * Q_BLOCK:(h + 1) * Q_BLOCK, :] = ql.astype(BF16)
    m_ref[...] = jnp.full(m_ref.shape, NEG, F32)

    def score_chunk(sub0, n_sb, n_bias=0):
        width = n_sb * Q_BLOCK
        kstart = pl.multiple_of(sub0 * Q_BLOCK, Q_BLOCK)
        kc = cn_ref[0, pl.ds(kstart, width), :]
        s = lax.dot_general(qlat_ref[...], kc, (((1,), (1,)), ((), ())),
                            preferred_element_type=F32)
        s3 = s.reshape(n_grp, Q_BLOCK, width)
        mx = m_ref[...]
        for j in range(n_sb):
            blk = s3[:, :, j * Q_BLOCK:(j + 1) * Q_BLOCK] + mask_ref[0, 0, sub0 + j].astype(F32)[None]
            jb = j - (n_sb - n_bias)
            if jb >= 0:
                lo = (NEAR_SUB - n_bias + jb) * Q_BLOCK
                blk = blk + bias_ref[:, :, lo:lo + Q_BLOCK]
            blk = blk.reshape(rows, Q_BLOCK)
            s_ref[sub0 + j] = blk
            mx = jnp.maximum(mx, blk)
        m_ref[...] = mx

    def quads_then_tail(n_far, chunk, finish):
        n_quad = n_far // 4

        def octet(j, carry):
            chunk(8 * j, 8, 0)
            return carry

        lax.fori_loop(0, n_quad // 2, octet, 0)

        @pl.when(n_quad % 2 == 1)
        def _():
            chunk(4 * (n_quad - 1), 4, 0)

        @pl.when(i == 0)
        def _():
            chunk(0, 1, 1)
            finish()

        for rem in range(4):
            @pl.when(jnp.logical_and(i > 0, n_far - 4 * n_quad == rem))
            def _(rem=rem):
                chunk(4 * n_quad, rem + NEAR_SUB, NEAR_SUB)
                finish()

    def finish_scores():
        m_ref[...] = jnp.broadcast_to(jnp.max(m_ref[...], axis=-1, keepdims=True), m_ref.shape)
        l_ref[...] = jnp.zeros(l_ref.shape, F32)
        acc_ref[...] = jnp.zeros(acc_ref.shape, F32)

    def pv_chunk(sub0, n_sb, n_bias=0):
        kstart = pl.multiple_of(sub0 * Q_BLOCK, Q_BLOCK)
        kc = cn_ref[0, pl.ds(kstart, n_sb * Q_BLOCK), :]
        m = m_ref[...]
        ps = [jnp.exp2(s_ref[sub0 + j] - m) for j in range(n_sb)]
        l_ref[...] += functools.reduce(lambda a, b: a + b, ps)
        p = jnp.concatenate([pj.astype(BF16) for pj in ps], axis=-1)
        acc_ref[...] += jnp.dot(p, kc, preferred_element_type=F32)

    def finish_output():
        l = jnp.sum(l_ref[...], axis=-1, keepdims=True)
        for h in range(n_grp):
            hs = slice(h * Q_BLOCK, (h + 1) * Q_BLOCK)
            oh = jnp.dot(acc_ref[hs, :].astype(BF16), wuv_ref[h],
                         preferred_element_type=F32) / l[hs]
            zz = z_ref[0, :, h * head_dim:(h + 1) * head_dim].astype(F32)
            y_ref[0, :, h * head_dim:(h + 1) * head_dim] = (oh * _silu(zz)).astype(BF16)

    n_far = jnp.maximum(i - 1, 0)
    quads_then_tail(n_far, score_chunk, finish_scores)
    quads_then_tail(n_far, pv_chunk, finish_output)


def _attention(qz, cn, mask, wuk_t, wuv_h, bias_tile, n_grp):
    b, seq, two_inner = qz.shape
    d_inner = two_inner // 2
    n_heads, head_dim, kv_lora = wuk_t.shape
    n_blk = seq // Q_BLOCK
    n_g = n_heads // n_grp
    gw = n_grp * head_dim
    rows = n_grp * Q_BLOCK
    return pl.pallas_call(
        functools.partial(_attn_kernel, n_grp=n_grp, head_dim=head_dim,
                          scale=head_dim ** -0.5 * LOG2E),
        grid=(n_g, b, n_blk),
        in_specs=[pl.BlockSpec((1, Q_BLOCK, gw), lambda g, bb, i: (bb, i, g)),
                  pl.BlockSpec((1, Q_BLOCK, gw), lambda g, bb, i: (bb, i, n_g + g)),
                  pl.BlockSpec((1, seq, kv_lora), lambda g, bb, i: (bb, 0, 0)),
                  pl.BlockSpec((1, 1, n_blk, Q_BLOCK, Q_BLOCK), lambda g, bb, i: (bb, i, 0, 0, 0)),
                  pl.BlockSpec((n_grp, head_dim, kv_lora), lambda g, bb, i: (g, 0, 0)),
                  pl.BlockSpec((n_grp, kv_lora, head_dim), lambda g, bb, i: (g, 0, 0)),
                  pl.BlockSpec((n_grp, Q_BLOCK, NEAR_SUB * Q_BLOCK), lambda g, bb, i: (g, 0, 0))],
        out_specs=pl.BlockSpec((1, Q_BLOCK, gw), lambda g, bb, i: (bb, i, g)),
        out_shape=jax.ShapeDtypeStruct((b, seq, d_inner), BF16),
        scratch_shapes=[pltpu.VMEM((rows, kv_lora), BF16),
                        pltpu.VMEM((n_blk, rows, Q_BLOCK), F32),
                        pltpu.VMEM((rows, kv_lora), F32),
                        pltpu.VMEM((rows, Q_BLOCK), F32),
                        pltpu.VMEM((rows, Q_BLOCK), F32)],
        compiler_params=_cparams(3),
        name="latent_attention",
    )(qz, qz, cn, mask, wuk_t, wuv_h, bias_tile * LOG2E)


def _t5_bucket_table(n):
    max_exact = REL_BUCKETS // 2
    d = np.arange(n)
    df = np.maximum(d, 1).astype(np.float64)
    large = max_exact + (np.log(df / max_exact) / math.log(REL_MAX_DIST / max_exact)
                         * (REL_BUCKETS - max_exact)).astype(np.int64)
    large = np.minimum(large, REL_BUCKETS - 1)
    return np.where(d < max_exact, d, large).astype(np.int32)


def _bias_tile(rel_bias):
    width = NEAR_SUB * Q_BLOCK
    period = width + Q_BLOCK
    table = _t5_bucket_table(period)
    assert (table[Q_BLOCK:] == REL_BUCKETS - 1).all()
    k = np.arange(period)
    dist = np.where(k <= width, (NEAR_SUB - 1) * Q_BLOCK - k, (NEAR_SUB - 1) * Q_BLOCK + period - k)
    w = rel_bias[table[np.maximum(dist, 0)]] - rel_bias[REL_BUCKETS - 1]
    flat = jnp.tile(w.T, (1, Q_BLOCK))[:, :Q_BLOCK * (period - 1)]
    return flat.reshape(-1, Q_BLOCK, period - 1)[:, :, :width].astype(F32)


def _pool_kernel(u_ref, z_ref, w_ref, b_ref, sc_ref, y_ref, *, tr):
    g = pl.program_id(1)
    for gi, window in enumerate(POOL_WINDOWS):
        @pl.when(g == gi)
        def _(window=window):
            _pool_group(u_ref, z_ref, w_ref, b_ref, sc_ref, y_ref, window, tr)


def _pool_group(u_ref, z_ref, w_ref, b_ref, sc_ref, y_ref, window, tr):
    seq = u_ref.shape[1]
    pg = u_ref.shape[2]

    def window_sum(ext):
        acc, span = ext, 1
        while span < window:
            acc = acc + pltpu.roll(acc, span, 0)
            span *= 2
        return acc[POOL_HALO:]

    def mix_and_gate(pooled, r0):
        mixed = jnp.dot(pooled.astype(BF16), w_ref[0], preferred_element_type=F32) + b_ref[0]
        mixed = mixed * sc_ref[0]
        zz = z_ref[0, pl.ds(r0, tr), :].astype(F32)
        y_ref[0, pl.ds(r0, tr), :] = (mixed * _silu(zz)).astype(BF16)

    first = jnp.concatenate([jnp.zeros((POOL_HALO, pg), F32), u_ref[0, 0:tr, :].astype(F32)],
                            axis=0)
    t = lax.broadcasted_iota(jnp.int32, (tr, 1), 0)
    cnt = jnp.minimum(t + 1, window).astype(F32)
    mix_and_gate(window_sum(first) / cnt - first[POOL_HALO:], 0)

    def body(r, carry):
        r0 = pl.multiple_of(r * tr, tr)
        ext = u_ref[0, pl.ds(pl.multiple_of(r0 - POOL_HALO, POOL_HALO), tr + POOL_HALO), :]
        ext = ext.astype(F32)
        mix_and_gate(window_sum(ext) * (1.0 / window) - ext[POOL_HALO:], r0)
        return carry

    lax.fori_loop(1, seq // tr, body, 0)


def _pool_mix(u, z, col0, w_grp, b_grp, scale, tr):
    b, seq, _ = u.shape
    n_groups, pg, _ = w_grp.shape
    d_inner = n_groups * pg
    zoff = col0 // pg
    return pl.pallas_call(
        functools.partial(_pool_kernel, tr=tr),
        grid=(b, n_groups),
        in_specs=[pl.BlockSpec((1, seq, pg), lambda bb, g: (bb, 0, g)),
                  pl.BlockSpec((1, seq, pg), lambda bb, g: (bb, 0, zoff + g)),
                  pl.BlockSpec((1, pg, pg), lambda bb, g: (g, 0, 0)),
                  pl.BlockSpec((1, 1, pg), lambda bb, g: (g, 0, 0)),
                  pl.BlockSpec((1, 1, pg), lambda bb, g: (g, 0, 0))],
        out_specs=pl.BlockSpec((1, seq, pg), lambda bb, g: (bb, 0, g)),
        out_shape=jax.ShapeDtypeStruct((b, seq, d_inner), BF16),
        compiler_params=_cparams(2),
        name="pool_mix",
    )(u, z, w_grp, b_grp.reshape(n_groups, 1, pg), scale.reshape(n_groups, 1, pg))


def _dsa_layer(h, norm_g, w_in, kv_norm, kidx_norm, w_uk, w_uv, w_out, rel_bias):
    b, seq, d_model = h.shape
    d_inner = w_out.shape[0]
    kv_lora, n_heads, head_dim = w_uk.shape
    idx_dim = kidx_norm.shape[0]
    a_in = w_in.shape[1]
    idx_heads = (a_in - 2 * d_inner - kv_lora - idx_dim) // (idx_dim + 1)
    o_ckv = d_inner
    o_qi = o_ckv + kv_lora
    o_ki = o_qi + idx_heads * idx_dim
    o_wi = o_ki + idx_dim
    o_z = o_wi + idx_heads
    assert o_z + d_inner == a_in and head_dim == Q_BLOCK and n_heads * head_dim == d_inner
    top_k = min(TOP_K_MAX, seq // 4)
    m = b * seq
    hm = h.reshape(m, d_model)

    w_qz = jnp.concatenate([w_in[:, :d_inner].astype(BF16), w_in[:, o_z:].astype(BF16)], axis=1)
    n_qi = idx_heads * idx_dim
    pad = (-(idx_dim + idx_heads)) % Q_BLOCK
    w_idx = jnp.concatenate([w_in[:, o_qi:o_ki].astype(BF16), w_in[:, o_ckv:o_qi].astype(BF16),
                             w_in[:, o_ki:o_z].astype(BF16), jnp.zeros((d_model, pad), BF16)],
                            axis=1)

    qz = _norm_matmul(hm, norm_g, w_qz, BF16, min(PROJ_ROWS, m), PROJ_COLS, "in_proj_a_qz")
    qi, cn, kn, kw = _idx_proj(hm, norm_g, w_idx, kv_norm, kidx_norm, n_qi, min(PROJ_ROWS, m))
    mask = _indexer(qi.reshape(b, seq, -1), kw.reshape(b, seq, -1), kn.reshape(b, seq, idx_dim),
                    top_k, idx_heads, idx_dim, idx_dim)

    wuk_t = jnp.transpose(w_uk, (1, 2, 0)).astype(BF16)
    wuv_h = jnp.transpose(w_uv, (1, 0, 2)).astype(BF16)
    n_grp = min(ATTN_HEADS, n_heads)
    y = _attention(qz.reshape(b, seq, 2 * d_inner), cn.reshape(b, seq, kv_lora), mask,
                   wuk_t, wuv_h, _bias_tile(rel_bias), n_grp)
    out = _matmul_res(y.reshape(m, d_inner), w_out.astype(BF16), hm, min(PROJ_ROWS, m), OUT_COLS,
                      "out_proj_a")
    return out.reshape(b, seq, d_model)


def _pool_layer(h, norm_g, w_in, w_grp, b_grp, scale, w_out, out_norm_g):
    b, seq, d_model = h.shape
    d_inner = w_out.shape[0]
    assert w_grp.shape[0] == len(POOL_WINDOWS) and POOL_WINDOWS == (2, 4, 8, 16)
    m = b * seq
    hm = h.reshape(m, d_model)
    uz = _norm_matmul(hm, norm_g, w_in.astype(BF16), BF16, min(PROJ_ROWS, m), PROJ_COLS,
                      "in_proj_b")
    uz = uz.reshape(b, seq, 2 * d_inner)
    y = _pool_mix(uz, uz, d_inner, w_grp.astype(BF16), b_grp, scale, min(POOL_ROWS, seq))
    y = y.reshape(m, d_inner)
    if out_norm_g is None:
        out = _matmul_res(y, w_out.astype(BF16), hm, min(PROJ_ROWS, m), OUT_COLS, "out_proj_b")
    else:
        out = _matmul_res_norm(y, w_out.astype(BF16), hm, out_norm_g, min(NORM_OUT_ROWS, m),
                               "out_proj_b_norm")
    return out.reshape(b, seq, d_model)


def kernel(x, norm_a, w_in_a, kv_norm_a, kidx_norm_a, w_uk_a, w_uv_a, w_out_a, norm_b, w_in_b,
           w_grp_b, b_grp_b, scale_b, w_out_b, rel_bias, final_norm):
    n_a = norm_a.shape[0]
    n_b = norm_b.shape[0]
    depth = n_a + n_b
    h = x
    for layer in range(depth):
        j = layer // 2
        if layer % 2 == 0:
            h = _dsa_layer(h, norm_a[j], w_in_a[j], kv_norm_a[j], kidx_norm_a[j], w_uk_a[j],
                           w_uv_a[j], w_out_a[j], rel_bias)
        else:
            h = _pool_layer(h, norm_b[j], w_in_b[j], w_grp_b[j], b_grp_b[j], scale_b[j],
                            w_out_b[j], final_norm if layer == depth - 1 else None)
    if depth % 2 == 0:
        return h
    b, seq, d_model = h.shape
    out = _rmsnorm(h.reshape(b * seq, d_model), final_norm, min(PROJ_ROWS, b * seq), "final_norm")
    return out.reshape(b, seq, d_model)
```

```python
import functools
import math

import numpy as np
import jax
import jax.numpy as jnp
from jax import lax
from jax.experimental import pallas as pl
from jax.experimental.pallas import tpu as pltpu

F32 = jnp.float32
BF16 = jnp.bfloat16

EPS = 1e-6
Q_BLOCK = 128
TOP_K_MAX = 256
REL_BUCKETS = 32
REL_MAX_DIST = 128
POOL_WINDOWS = (2, 4, 8, 16)
POOL_HALO = 16
NEG = -1e30
INT_MIN = -2 ** 31
LOG2E = math.log2(math.e)
NEAR_SUB = 2
ROW_ALIGN = 16

VMEM_LIMIT = 56 * 1024 * 1024
PROJ_ROWS = 1024
PROJ_COLS = 2048
OUT_COLS = 1024
NORM_OUT_ROWS = 512
POOL_ROWS = 1024
INDEXER_ROWS = 256
INDEXER_SPAN = 256
ATTN_HEADS = 16


def _cparams(n_grid):
    return pltpu.CompilerParams(dimension_semantics=("arbitrary",) * n_grid,
                                vmem_limit_bytes=VMEM_LIMIT)


def _silu(z):
    return z * (1.0 / (1.0 + jnp.exp(-z)))


def _norm_matmul_kernel(x_ref, g_ref, w_ref, o_ref, xn_ref, *, w_rows_are_outputs):
    @pl.when(pl.program_id(1) == 0)
    def _():
        x = x_ref[...]
        ms = jnp.mean(x * x, axis=-1, keepdims=True)
        xn_ref[...] = (x * lax.rsqrt(ms + EPS) * g_ref[...]).astype(BF16)

    contract_w = 1 if w_rows_are_outputs else 0
    o_ref[...] = lax.dot_general(xn_ref[...], w_ref[...], (((1,), (contract_w,)), ((), ())),
                                 preferred_element_type=F32).astype(o_ref.dtype)


def _norm_matmul(x, g, w, out_dtype, tm, tn, name, w_rows_are_outputs=False, row_spans=None):
    m, k = x.shape
    if row_spans is not None:
        n = sum(hi - lo for lo, hi in row_spans)
        tn = min([tn] + [hi - lo for lo, hi in row_spans])
        assert all((hi - lo) % tn == 0 for lo, hi in row_spans)

        def row_start(i, j):
            start, first = jnp.int32(0), 0
            for lo, hi in row_spans:
                start = jnp.where(j >= first, lo + (j - first) * tn, start)
                first += (hi - lo) // tn
            return pl.multiple_of(start, ROW_ALIGN), 0

        assert all(lo % ROW_ALIGN == 0 for lo, _ in row_spans)
        w_spec = pl.BlockSpec((pl.Element(tn), pl.Element(k)), row_start)
    else:
        n = w.shape[0] if w_rows_are_outputs else w.shape[1]
        tn = min(tn, n)
        w_spec = (pl.BlockSpec((tn, k), lambda i, j: (j, 0)) if w_rows_are_outputs
                  else pl.BlockSpec((k, tn), lambda i, j: (0, j)))
    return pl.pallas_call(
        functools.partial(_norm_matmul_kernel, w_rows_are_outputs=w_rows_are_outputs),
        grid=(m // tm, n // tn),
        in_specs=[pl.BlockSpec((tm, k), lambda i, j: (i, 0)),
                  pl.BlockSpec((1, k), lambda i, j: (0, 0)),
                  w_spec],
        out_specs=pl.BlockSpec((tm, tn), lambda i, j: (i, j)),
        out_shape=jax.ShapeDtypeStruct((m, n), out_dtype),
        scratch_shapes=[pltpu.VMEM((tm, k), BF16)],
        compiler_params=_cparams(2),
        name=name,
    )(x, g.reshape(1, k), w)


def _matmul_res_kernel(y_ref, w_ref, r_ref, o_ref):
    o_ref[...] = jnp.dot(y_ref[...], w_ref[...], preferred_element_type=F32) + r_ref[...]


def _matmul_res(y, w, r, tm, tn, name):
    m, k = y.shape
    n = w.shape[1]
    tn = min(tn, n)
    return pl.pallas_call(
        _matmul_res_kernel,
        grid=(n // tn, m // tm),
        in_specs=[pl.BlockSpec((tm, k), lambda j, i: (i, 0)),
                  pl.BlockSpec((k, tn), lambda j, i: (0, j)),
                  pl.BlockSpec((tm, tn), lambda j, i: (i, j))],
        out_specs=pl.BlockSpec((tm, tn), lambda j, i: (i, j)),
        out_shape=jax.ShapeDtypeStruct((m, n), F32),
        compiler_params=_cparams(2),
        name=name,
    )(y, w, r)


def _matmul_res_norm_kernel(y_ref, w_ref, r_ref, g_ref, o_ref):
    h = jnp.dot(y_ref[...], w_ref[...], preferred_element_type=F32) + r_ref[...]
    ms = jnp.mean(h * h, axis=-1, keepdims=True)
    o_ref[...] = h * lax.rsqrt(ms + EPS) * g_ref[...]


def _matmul_res_norm(y, w, r, g, tm, name):
    m, k = y.shape
    n = w.shape[1]
    return pl.pallas_call(
        _matmul_res_norm_kernel,
        grid=(m // tm,),
        in_specs=[pl.BlockSpec((tm, k), lambda i: (i, 0)),
                  pl.BlockSpec((k, n), lambda i: (0, 0), pipeline_mode=pl.Buffered(1)),
                  pl.BlockSpec((tm, n), lambda i: (i, 0)),
                  pl.BlockSpec((1, n), lambda i: (0, 0))],
        out_specs=pl.BlockSpec((tm, n), lambda i: (i, 0)),
        out_shape=jax.ShapeDtypeStruct((m, n), F32),
        compiler_params=_cparams(1),
        name=name,
    )(y, w, r, g.reshape(1, n))


def _rmsnorm_kernel(x_ref, g_ref, o_ref):
    x = x_ref[...]
    ms = jnp.mean(x * x, axis=-1, keepdims=True)
    o_ref[...] = x * lax.rsqrt(ms + EPS) * g_ref[...]


def _rmsnorm(x, g, tm, name):
    m, k = x.shape
    return pl.pallas_call(
        _rmsnorm_kernel,
        grid=(m // tm,),
        in_specs=[pl.BlockSpec((tm, k), lambda i: (i, 0)),
                  pl.BlockSpec((1, k), lambda i: (0, 0))],
        out_specs=pl.BlockSpec((tm, k), lambda i: (i, 0)),
        out_shape=jax.ShapeDtypeStruct((m, k), F32),
        compiler_params=_cparams(1),
        name=name,
    )(x, g.reshape(1, k))


def _idx_proj_kernel(x_ref, g_ref, w_ref, kvn_ref, kin_ref, qi_ref, cn_ref, kn_ref, kw_ref):
    n_qi = qi_ref.shape[1]
    kv_lora = cn_ref.shape[1]
    idx_dim = kn_ref.shape[1]
    x = x_ref[...]
    ms = jnp.mean(x * x, axis=-1, keepdims=True)
    xn = (x * lax.rsqrt(ms + EPS) * g_ref[...]).astype(BF16)
    p = lax.dot_general(xn, w_ref[...], (((1,), (1,)), ((), ())),
                        preferred_element_type=F32)
    qi_ref[...] = p[:, :n_qi]
    c = p[:, n_qi:n_qi + kv_lora]
    kw = p[:, n_qi + kv_lora:]
    k = kw[:, :idx_dim]
    cn = c * lax.rsqrt(jnp.mean(c * c, axis=-1, keepdims=True) + EPS) * kvn_ref[...]
    kn = k * lax.rsqrt(jnp.mean(k * k, axis=-1, keepdims=True) + EPS) * kin_ref[...]
    cn_ref[...] = cn.astype(BF16)
    kn_ref[...] = kn.astype(BF16)
    kw_ref[...] = kw


def _idx_proj(x, g, w, kv_norm, kidx_norm, n_qi, tm):
    m, k = x.shape
    n = w.shape[0]
    kv_lora = kv_norm.shape[0]
    idx_dim = kidx_norm.shape[0]
    n_kw = n - n_qi - kv_lora
    return pl.pallas_call(
        _idx_proj_kernel,
        grid=(m // tm,),
        in_specs=[pl.BlockSpec((tm, k), lambda i: (i, 0)),
                  pl.BlockSpec((1, k), lambda i: (0, 0)),
                  pl.BlockSpec((n, k), lambda i: (0, 0)),
                  pl.BlockSpec((1, kv_lora), lambda i: (0, 0)),
                  pl.BlockSpec((1, idx_dim), lambda i: (0, 0))],
        out_specs=[pl.BlockSpec((tm, n_qi), lambda i: (i, 0)),
                   pl.BlockSpec((tm, kv_lora), lambda i: (i, 0)),
                   pl.BlockSpec((tm, idx_dim), lambda i: (i, 0)),
                   pl.BlockSpec((tm, n_kw), lambda i: (i, 0))],
        out_shape=[jax.ShapeDtypeStruct((m, n_qi), F32),
                   jax.ShapeDtypeStruct((m, kv_lora), BF16),
                   jax.ShapeDtypeStruct((m, idx_dim), BF16),
                   jax.ShapeDtypeStruct((m, n_kw), F32)],
        compiler_params=_cparams(1),
        name="in_proj_a_idx",
    )(x, g.reshape(1, k), w, kv_norm.reshape(1, kv_lora), kidx_norm.reshape(1, idx_dim))


def _indexer_kernel(qi_ref, sm_ref, kn_ref, mask_ref, *, span, **kw):
    i = pl.program_id(1)
    rows = qi_ref.shape[1]
    total = kn_ref.shape[1]
    for v in range(1, total // span + 1):
        @pl.when(((i + 1) * rows + span - 1) // span == v)
        def _(v=v):
            _indexer_extent(qi_ref, sm_ref, kn_ref, mask_ref, v * span, **kw)


def _indexer_extent(qi_ref, sm_ref, kn_ref, mask_ref, seq, *, top_k, n_heads, idx_dim, w_off):
    i = pl.program_id(1)
    rows = qi_ref.shape[1]
    n_sub = seq // Q_BLOCK
    q = qi_ref[0]
    w = sm_ref[0][:, w_off:w_off + n_heads] * (n_heads ** -0.5 * idx_dim ** -0.5)
    kn = kn_ref[0, :seq, :]

    score = jnp.zeros((rows, seq), F32)
    for h in range(n_heads):
        qh = q[:, h * idx_dim:(h + 1) * idx_dim].astype(BF16)
        s = lax.dot_general(qh, kn, (((1,), (1,)), ((), ())), preferred_element_type=F32)
        score = score + w[:, h:h + 1] * jnp.maximum(s, 0.0)

    bits = pltpu.bitcast(score + 0.0, jnp.int32)
    key = jnp.where(bits >= 0, bits, bits ^ jnp.int32(0x7FFFFFFF))
    t_pos = i * rows + lax.broadcasted_iota(jnp.int32, (rows, seq), 0)
    s_pos = lax.broadcasted_iota(jnp.int32, (rows, seq), 1)
    causal = s_pos <= t_pos
    key = jnp.where(causal, key, jnp.int32(INT_MIN))

    key_t = key.T

    def count_ge(thr_row):
        return jnp.sum(jnp.where(key_t >= thr_row, 1.0, 0.0), axis=0, keepdims=True)

    kf = float(top_k)
    thr_row = jnp.where(count_ge(jnp.zeros((1, rows), jnp.int32)) >= kf,
                        jnp.int32(0), jnp.int32(INT_MIN))

    def bit_step(b, thr_row):
        cand = thr_row + jnp.left_shift(jnp.int32(1), 30 - b)
        return jnp.where(count_ge(cand) >= kf, cand, thr_row)

    thr_row = lax.fori_loop(0, 31, bit_step, thr_row)
    thr = jnp.broadcast_to(thr_row, (Q_BLOCK, rows)).T[:, :1]

    ge = key >= thr
    sel = jnp.where(causal, jnp.where(ge, 1.0, 0.0), 0.0)
    def write_mask(j, cols):
        for r in range(rows // Q_BLOCK):
            mask_ref[0, r, j] = cols[r * Q_BLOCK:(r + 1) * Q_BLOCK]

    for j in range(n_sub):
        blk = sel[:, j * Q_BLOCK:(j + 1) * Q_BLOCK]
        write_mask(j, jnp.where(blk > 0.5, 0.0, NEG).astype(BF16))
    for j in range(n_sub, mask_ref.shape[2]):
        write_mask(j, jnp.full((rows, Q_BLOCK), NEG, BF16))

    n_ge = jnp.sum(jnp.where(ge, 1.0, 0.0), axis=-1, keepdims=True)
    has_tie = jnp.logical_and(n_ge > kf, thr > jnp.int32(INT_MIN))
    any_tie = jnp.max(jnp.where(has_tie, 1.0, 0.0))

    @pl.when(any_tie > 0.0)
    def _():
        cw = 2 * Q_BLOCK
        gt = key > thr
        eq = key == thr
        need = kf - jnp.sum(jnp.where(gt, 1.0, 0.0), axis=-1, keepdims=True)
        r_id = lax.broadcasted_iota(jnp.int32, (cw, cw), 0)
        c_id = lax.broadcasted_iota(jnp.int32, (cw, cw), 1)
        upper = jnp.where(r_id < c_id, 1.0, 0.0).astype(BF16)
        carry = jnp.zeros((rows, 1), F32)
        for c in range(seq // cw):
            sl = slice(c * cw, (c + 1) * cw)
            eqc = jnp.where(eq[:, sl], 1.0, 0.0)
            rank = carry + jnp.dot(eqc.astype(BF16), upper, preferred_element_type=F32)
            keep = jnp.where(gt[:, sl], 1.0, jnp.where(rank < need, eqc, 0.0))
            keep = jnp.where(has_tie, keep, sel[:, sl])
            carry = carry + jnp.sum(eqc, axis=-1, keepdims=True)
            add = jnp.where(keep > 0.5, 0.0, NEG).astype(BF16)
            write_mask(2 * c, add[:, :Q_BLOCK])
            write_mask(2 * c + 1, add[:, Q_BLOCK:])


def _indexer(qi, small, kn, top_k, n_heads, idx_dim, w_off):
    b, seq, _ = qi.shape
    n_blk = seq // Q_BLOCK
    rows = min(INDEXER_ROWS, seq)
    return pl.pallas_call(
        functools.partial(_indexer_kernel, span=min(INDEXER_SPAN, seq), top_k=top_k,
                          n_heads=n_heads, idx_dim=idx_dim, w_off=w_off),
        grid=(b, seq // rows),
        in_specs=[pl.BlockSpec((1, rows, qi.shape[2]), lambda bb, i: (bb, i, 0)),
                  pl.BlockSpec((1, rows, small.shape[2]), lambda bb, i: (bb, i, 0)),
                  pl.BlockSpec((1, seq, idx_dim), lambda bb, i: (bb, 0, 0))],
        out_specs=pl.BlockSpec((1, rows // Q_BLOCK, n_blk, Q_BLOCK, Q_BLOCK),
                               lambda bb, i: (bb, i, 0, 0, 0)),
        out_shape=jax.ShapeDtypeStruct((b, n_blk, n_blk, Q_BLOCK, Q_BLOCK), BF16),
        compiler_params=_cparams(2),
        name="indexer_mask",
    )(qi, small, kn)


def _attn_kernel(q_ref, z_ref, cn_ref, mask_ref, wuk_ref, wuv_ref, bias_ref, y_ref,
                 qlat_ref, s_ref, acc_ref, m_ref, l_ref, *, n_grp, head_dim, scale):
    i = pl.program_id(2)
    rows = n_grp * Q_BLOCK

    for h in range(n_grp):
        qh = q_ref[0, :, h * head_dim:(h + 1) * head_dim]
        ql = jnp.dot(qh, wuk_ref[h], preferred_element_type=F32) * scale
        qlat_ref[h * Q_BLOCK:(h + 1) * Q_BLOCK, :] = ql.astype(BF16)
    m_ref[...] = jnp.full(m_ref.shape, NEG, F32)

    def score_chunk(sub0, n_sb, n_bias=0):
        width = n_sb * Q_BLOCK
        kstart = pl.multiple_of(sub0 * Q_BLOCK, Q_BLOCK)
        kc = cn_ref[0, pl.ds(kstart, width), :]
        s = lax.dot_general(qlat_ref[...], kc, (((1,), (1,)), ((), ())),
                            preferred_element_type=F32)
        s3 = s.reshape(n_grp, Q_BLOCK, width)
        mx = m_ref[...]
        for j in range(n_sb):
            blk = s3[:, :, j * Q_BLOCK:(j + 1) * Q_BLOCK] + mask_ref[0, 0, sub0 + j].astype(F32)[None]
            jb = j - (n_sb - n_bias)
            if jb >= 0:
                lo = (NEAR_SUB - n_bias + jb) * Q_BLOCK
                blk = blk + bias_ref[:, :, lo:lo + Q_BLOCK]
            blk = blk.reshape(rows, Q_BLOCK)
            s_ref[sub0 + j] = blk
            mx = jnp.maximum(mx, blk)
        m_ref[...] = mx

    def quads_then_tail(n_far, chunk, finish):
        n_quad = n_far // 4

        def octet(j, carry):
            chunk(8 * j, 8, 0)
            return carry

        lax.fori_loop(0, n_quad // 2, octet, 0)

        @pl.when(n_quad % 2 == 1)
        def _():
            chunk(4 * (n_quad - 1), 4, 0)

        @pl.when(i == 0)
        def _():
            chunk(0, 1, 1)
            finish()

        for rem in range(4):
            @pl.when(jnp.logical_and(i > 0, n_far - 4 * n_quad == rem))
            def _(rem=rem):
                chunk(4 * n_quad, rem + NEAR_SUB, NEAR_SUB)
                finish()

    def finish_scores():
        m_ref[...] = jnp.broadcast_to(jnp.max(m_ref[...], axis=-1, keepdims=True), m_ref.shape)
        l_ref[...] = jnp.zeros(l_ref.shape, F32)
        acc_ref[...] = jnp.zeros(acc_ref.shape, F32)

    def pv_chunk(sub0, n_sb, n_bias=0):
        kstart = pl.multiple_of(sub0 * Q_BLOCK, Q_BLOCK)
        kc = cn_ref[0, pl.ds(kstart, n_sb * Q_BLOCK), :]
        m = m_ref[...]
        ps = [jnp.exp2(s_ref[sub0 + j] - m) for j in range(n_sb)]
        l_ref[...] += functools.reduce(lambda a, b: a + b, ps)
        p = jnp.concatenate([pj.astype(BF16) for pj in ps], axis=-1)
        acc_ref[...] += jnp.dot(p, kc, preferred_element_type=F32)

    def finish_output():
        l = jnp.sum(l_ref[...], axis=-1, keepdims=True)
        for h in range(n_grp):
            hs = slice(h * Q_BLOCK, (h + 1) * Q_BLOCK)
            oh = jnp.dot(acc_ref[hs, :].astype(BF16), wuv_ref[h],
                         preferred_element_type=F32) / l[hs]
            zz = z_ref[0, :, h * head_dim:(h + 1) * head_dim].astype(F32)
            y_ref[0, :, h * head_dim:(h + 1) * head_dim] = (oh * _silu(zz)).astype(BF16)

    n_far = jnp.maximum(i - 1, 0)
    quads_then_tail(n_far, score_chunk, finish_scores)
    quads_then_tail(n_far, pv_chunk, finish_output)


def _attention(qz, cn, mask, wuk_t, wuv_h, bias_tile, n_grp):
    b, seq, two_inner = qz.shape
    d_inner = two_inner // 2
    n_heads, head_dim, kv_lora = wuk_t.shape
    n_blk = seq // Q_BLOCK
    n_g = n_heads // n_grp
    gw = n_grp * head_dim
    rows = n_grp * Q_BLOCK
    return pl.pallas_call(
        functools.partial(_attn_kernel, n_grp=n_grp, head_dim=head_dim,
                          scale=head_dim ** -0.5 * LOG2E),
        grid=(n_g, b, n_blk),
        in_specs=[pl.BlockSpec((1, Q_BLOCK, gw), lambda g, bb, i: (bb, i, g)),
                  pl.BlockSpec((1, Q_BLOCK, gw), lambda g, bb, i: (bb, i, n_g + g)),
                  pl.BlockSpec((1, seq, kv_lora), lambda g, bb, i: (bb, 0, 0)),
                  pl.BlockSpec((1, 1, n_blk, Q_BLOCK, Q_BLOCK), lambda g, bb, i: (bb, i, 0, 0, 0)),
                  pl.BlockSpec((n_grp, head_dim, kv_lora), lambda g, bb, i: (g, 0, 0)),
                  pl.BlockSpec((n_grp, kv_lora, head_dim), lambda g, bb, i: (g, 0, 0)),
                  pl.BlockSpec((n_grp, Q_BLOCK, NEAR_SUB * Q_BLOCK), lambda g, bb, i: (g, 0, 0))],
        out_specs=pl.BlockSpec((1, Q_BLOCK, gw), lambda g, bb, i: (bb, i, g)),
        out_shape=jax.ShapeDtypeStruct((b, seq, d_inner), BF16),
        scratch_shapes=[pltpu.VMEM((rows, kv_lora), BF16),
                        pltpu.VMEM((n_blk, rows, Q_BLOCK), F32),
                        pltpu.VMEM((rows, kv_lora), F32),
                        pltpu.VMEM((rows, Q_BLOCK), F32),
                        pltpu.VMEM((rows, Q_BLOCK), F32)],
        compiler_params=_cparams(3),
        name="latent_attention",
    )(qz, qz, cn, mask, wuk_t, wuv_h, bias_tile * LOG2E)


def _t5_bucket_table(n):
    max_exact = REL_BUCKETS // 2
    d = np.arange(n)
    df = np.maximum(d, 1).astype(np.float64)
    large = max_exact + (np.log(df / max_exact) / math.log(REL_MAX_DIST / max_exact)
                         * (REL_BUCKETS - max_exact)).astype(np.int64)
    large = np.minimum(large, REL_BUCKETS - 1)
    return np.where(d < max_exact, d, large).astype(np.int32)


def _bias_tile(rel_bias):
    width = NEAR_SUB * Q_BLOCK
    period = width + Q_BLOCK
    table = _t5_bucket_table(period)
    assert (table[Q_BLOCK:] == REL_BUCKETS - 1).all()
    k = np.arange(period)
    dist = np.where(k <= width, (NEAR_SUB - 1) * Q_BLOCK - k, (NEAR_SUB - 1) * Q_BLOCK + period - k)
    w = rel_bias[table[np.maximum(dist, 0)]] - rel_bias[REL_BUCKETS - 1]
    flat = jnp.tile(w.T, (1, Q_BLOCK))[:, :Q_BLOCK * (period - 1)]
    return flat.reshape(-1, Q_BLOCK, period - 1)[:, :, :width].astype(F32)


def _pool_kernel(u_ref, z_ref, w_ref, b_ref, sc_ref, y_ref, *, tr):
    g = pl.program_id(1)
    for gi, window in enumerate(POOL_WINDOWS):
        @pl.when(g == gi)
        def _(window=window):
            _pool_group(u_ref, z_ref, w_ref, b_ref, sc_ref, y_ref, window, tr)


def _pool_group(u_ref, z_ref, w_ref, b_ref, sc_ref, y_ref, window, tr):
    seq = u_ref.shape[1]
    pg = u_ref.shape[2]

    def window_sum(ext):
        acc, span = ext, 1
        while span < window:
            acc = acc + pltpu.roll(acc, span, 0)
            span *= 2
        return acc[POOL_HALO:]

    def mix_and_gate(pooled, r0):
        mixed = jnp.dot(pooled.astype(BF16), w_ref[0], preferred_element_type=F32) + b_ref[0]
        mixed = mixed * sc_ref[0]
        zz = z_ref[0, pl.ds(r0, tr), :].astype(F32)
        y_ref[0, pl.ds(r0, tr), :] = (mixed * _silu(zz)).astype(BF16)

    first = jnp.concatenate([jnp.zeros((POOL_HALO, pg), F32), u_ref[0, 0:tr, :].astype(F32)],
                            axis=0)
    t = lax.broadcasted_iota(jnp.int32, (tr, 1), 0)
    cnt = jnp.minimum(t + 1, window).astype(F32)
    mix_and_gate(window_sum(first) / cnt - first[POOL_HALO:], 0)

    def body(r, carry):
        r0 = pl.multiple_of(r * tr, tr)
        ext = u_ref[0, pl.ds(pl.multiple_of(r0 - POOL_HALO, POOL_HALO), tr + POOL_HALO), :]
        ext = ext.astype(F32)
        mix_and_gate(window_sum(ext) * (1.0 / window) - ext[POOL_HALO:], r0)
        return carry

    lax.fori_loop(1, seq // tr, body, 0)


def _pool_mix(u, z, col0, w_grp, b_grp, scale, tr):
    b, seq, _ = u.shape
    n_groups, pg, _ = w_grp.shape
    d_inner = n_groups * pg
    zoff = col0 // pg
    return pl.pallas_call(
        functools.partial(_pool_kernel, tr=tr),
        grid=(b, n_groups),
        in_specs=[pl.BlockSpec((1, seq, pg), lambda bb, g: (bb, 0, g)),
                  pl.BlockSpec((1, seq, pg), lambda bb, g: (bb, 0, zoff + g)),
                  pl.BlockSpec((1, pg, pg), lambda bb, g: (g, 0, 0)),
                  pl.BlockSpec((1, 1, pg), lambda bb, g: (g, 0, 0)),
                  pl.BlockSpec((1, 1, pg), lambda bb, g: (g, 0, 0))],
        out_specs=pl.BlockSpec((1, seq, pg), lambda bb, g: (bb, 0, g)),
        out_shape=jax.ShapeDtypeStruct((b, seq, d_inner), BF16),
        compiler_params=_cparams(2),
        name="pool_mix",
    )(u, z, w_grp, b_grp.reshape(n_groups, 1, pg), scale.reshape(n_groups, 1, pg))


def _dsa_layer(h, norm_g, w_in, kv_norm, kidx_norm, w_uk, w_uv, w_out, rel_bias):
    b, seq, d_model = h.shape
    d_inner = w_out.shape[0]
    kv_lora, n_heads, head_dim = w_uk.shape
    idx_dim = kidx_norm.shape[0]
    a_in = w_in.shape[1]
    idx_heads = (a_in - 2 * d_inner - kv_lora - idx_dim) // (idx_dim + 1)
    o_ckv = d_inner
    o_qi = o_ckv + kv_lora
    o_ki = o_qi + idx_heads * idx_dim
    o_wi = o_ki + idx_dim
    o_z = o_wi + idx_heads
    assert o_z + d_inner == a_in and head_dim == Q_BLOCK and n_heads * head_dim == d_inner
    top_k = min(TOP_K_MAX, seq // 4)
    m = b * seq
    hm = h.reshape(m, d_model)

    n_qi = idx_heads * idx_dim
    pad = (-(idx_dim + idx_heads)) % Q_BLOCK
    wt = jnp.swapaxes(w_in, 0, 1).astype(BF16)
    wt_idx = jnp.concatenate([wt[o_qi:o_ki], wt[o_ckv:o_qi], wt[o_ki:o_z],
                              jnp.zeros((pad, d_model), BF16)], axis=0)

    qz = _norm_matmul(hm, norm_g, wt, BF16, min(PROJ_ROWS, m), PROJ_COLS, "in_proj_a_qz",
                      w_rows_are_outputs=True, row_spans=((0, d_inner), (o_z, a_in)))
    qi, cn, kn, kw = _idx_proj(hm, norm_g, wt_idx, kv_norm, kidx_norm, n_qi, min(PROJ_ROWS, m))
    mask = _indexer(qi.reshape(b, seq, -1), kw.reshape(b, seq, -1), kn.reshape(b, seq, idx_dim),
                    top_k, idx_heads, idx_dim, idx_dim)

    wuk_t = jnp.transpose(w_uk, (1, 2, 0)).astype(BF16)
    wuv_h = jnp.transpose(w_uv, (1, 0, 2)).astype(BF16)
    n_grp = min(ATTN_HEADS, n_heads)
    y = _attention(qz.reshape(b, seq, 2 * d_inner), cn.reshape(b, seq, kv_lora), mask,
                   wuk_t, wuv_h, _bias_tile(rel_bias), n_grp)
    out = _matmul_res(y.reshape(m, d_inner), w_out.astype(BF16), hm, min(PROJ_ROWS, m), OUT_COLS,
                      "out_proj_a")
    return out.reshape(b, seq, d_model)


def _pool_layer(h, norm_g, w_in, w_grp, b_grp, scale, w_out, out_norm_g):
    b, seq, d_model = h.shape
    d_inner = w_out.shape[0]
    assert w_grp.shape[0] == len(POOL_WINDOWS) and POOL_WINDOWS == (2, 4, 8, 16)
    m = b * seq
    hm = h.reshape(m, d_model)
    uz = _norm_matmul(hm, norm_g, w_in.astype(BF16), BF16, min(PROJ_ROWS, m), PROJ_COLS,
                      "in_proj_b")
    uz = uz.reshape(b, seq, 2 * d_inner)
    y = _pool_mix(uz, uz, d_inner, w_grp.astype(BF16), b_grp, scale, min(POOL_ROWS, seq))
    y = y.reshape(m, d_inner)
    if out_norm_g is None:
        out = _matmul_res(y, w_out.astype(BF16), hm, min(PROJ_ROWS, m), OUT_COLS, "out_proj_b")
    else:
        out = _matmul_res_norm(y, w_out.astype(BF16), hm, out_norm_g, min(NORM_OUT_ROWS, m),
                               "out_proj_b_norm")
    return out.reshape(b, seq, d_model)


def kernel(x, norm_a, w_in_a, kv_norm_a, kidx_norm_a, w_uk_a, w_uv_a, w_out_a, norm_b, w_in_b,
           w_grp_b, b_grp_b, scale_b, w_out_b, rel_bias, final_norm):
    n_a = norm_a.shape[0]
    n_b = norm_b.shape[0]
    depth = n_a + n_b
    h = x
    for layer in range(depth):
        j = layer // 2
        if layer % 2 == 0:
            h = _dsa_layer(h, norm_a[j], w_in_a[j], kv_norm_a[j], kidx_norm_a[j], w_uk_a[j],
                           w_uv_a[j], w_out_a[j], rel_bias)
        else:
            h = _pool_layer(h, norm_b[j], w_in_b[j], w_grp_b[j], b_grp_b[j], scale_b[j],
                            w_out_b[j], final_norm if layer == depth - 1 else None)
    if depth % 2 == 0:
        return h
    b, seq, d_model = h.shape
    out = _rmsnorm(h.reshape(b * seq, d_model), final_norm, min(PROJ_ROWS, b * seq), "final_norm")
    return out.reshape(b, seq, d_model)
```
